```python
import jax, jax.numpy as jnp
from jax import lax
import numpy as np

D_MODEL = 2048
BATCH = 16
SEQ = 2048
DEPTH = 1

ATTN_HEADS = 8
HEAD_DIM = 128
ATTN_WIDTH = ATTN_HEADS * HEAD_DIM
CONV_WIDTH = D_MODEL - ATTN_WIDTH
IN_WIDTH = 3 * ATTN_WIDTH + 2 * CONV_WIDTH
CONV_KERNEL = 31
MOBA_BLOCK = 256
MOBA_TOPK = 3
Q_CHUNK = 8
ROPE_THETA = 10000.0
D_FF = -(-(8 * D_MODEL) // (3 * 256)) * 256
N_MOD = 6
EPS = 1e-6

kernel_name = "hymba_moba_conformer_adaln_block"


def rms_norm(x, g):
    xf = x.astype(jnp.float32)
    y = xf * lax.rsqrt(jnp.mean(xf * xf, axis=-1, keepdims=True) + EPS)
    return (y * g.astype(jnp.float32)).astype(x.dtype)


def modulate(h, shift, scale):
    return h * (1 + scale[:, None, :]) + shift[:, None, :]


def rope(x, pos):
    half = x.shape[-1] // 2
    inv = ROPE_THETA ** (-jnp.arange(half, dtype=jnp.float32) / half)
    ang = pos[:, None] * inv[None, :]
    cos, sin = jnp.cos(ang), jnp.sin(ang)
    xf = x.astype(jnp.float32)
    x1, x2 = xf[..., :half], xf[..., half:]
    out = jnp.concatenate([x1 * cos - x2 * sin, x2 * cos + x1 * sin], axis=-1)
    return out.astype(x.dtype)


def conv_module(a, b, w, bias, ln_g, ln_b):
    u = a * jax.nn.sigmoid(b)
    y = lax.conv_general_dilated(
        u, w[:, None, :].astype(u.dtype), window_strides=(1,),
        padding=[(CONV_KERNEL - 1, 0)],
        dimension_numbers=("NWC", "WIO", "NWC"),
        feature_group_count=CONV_WIDTH) + bias
    yf = y.astype(jnp.float32)
    mu = jnp.mean(yf, axis=-1, keepdims=True)
    var = jnp.mean(jnp.square(yf - mu), axis=-1, keepdims=True)
    yn = (yf - mu) * lax.rsqrt(var + EPS) * ln_g.astype(jnp.float32) + ln_b.astype(jnp.float32)
    return jax.nn.silu(yn).astype(a.dtype)


def moba_attention(q, k, v):
    B, H, T, D = q.shape
    nb = -(-T // MOBA_BLOCK)
    pad = nb * MOBA_BLOCK - T
    kp = jnp.pad(k, ((0, 0), (0, 0), (0, pad), (0, 0)))
    vp = jnp.pad(v, ((0, 0), (0, 0), (0, pad), (0, 0)))
    k_blocks = kp.reshape(B, H, nb, MOBA_BLOCK, D)
    v_blocks = vp.reshape(B, H, nb, MOBA_BLOCK, D)
    k_mean = jnp.mean(k_blocks.astype(jnp.float32), axis=3)
    topk = min(MOBA_TOPK, nb - 1)
    scale = HEAD_DIM ** -0.5
    neg = jnp.finfo(jnp.float32).min
    b_idx = jnp.arange(B)[:, None, None, None]
    h_idx = jnp.arange(H)[None, :, None, None]

    def chunk(start):
        qc = lax.dynamic_slice_in_dim(q, start, Q_CHUNK, axis=2)
        blk = start // MOBA_BLOCK
        q_pos = start + jnp.arange(Q_CHUNK)
        k_pos = blk * MOBA_BLOCK + jnp.arange(MOBA_BLOCK)
        k_own = lax.dynamic_index_in_dim(k_blocks, blk, axis=2, keepdims=False)
        v_own = lax.dynamic_index_in_dim(v_blocks, blk, axis=2, keepdims=False)
        s_own = jnp.einsum("bhqd,bhkd->bhqk", qc, k_own).astype(jnp.float32) * scale
        s_own = jnp.where(k_pos[None, :] <= q_pos[:, None], s_own, neg)
        if topk > 0:
            gate = jnp.einsum("bhqd,bhnd->bhqn", qc.astype(jnp.float32), k_mean)
            gate = jnp.where(jnp.arange(nb) < blk, gate, -jnp.inf)
            _, sel = lax.top_k(gate, topk)
            k_sel = k_blocks[b_idx, h_idx, sel]
            v_sel = v_blocks[b_idx, h_idx, sel]
            s_sel = jnp.einsum("bhqd,bhqnkd->bhqnk", qc, k_sel).astype(jnp.float32) * scale
            s_sel = jnp.where((sel < blk)[..., None], s_sel, neg)
            s_sel = s_sel.reshape(B, H, Q_CHUNK, topk * MOBA_BLOCK)
            p = jax.nn.softmax(jnp.concatenate([s_own, s_sel], axis=-1), axis=-1)
            p_own = p[..., :MOBA_BLOCK].astype(v.dtype)
            p_sel = p[..., MOBA_BLOCK:].reshape(B, H, Q_CHUNK, topk, MOBA_BLOCK).astype(v.dtype)
            out = (jnp.einsum("bhqk,bhkd->bhqd", p_own, v_own)
                   + jnp.einsum("bhqnk,bhqnkd->bhqd", p_sel, v_sel))
        else:
            p = jax.nn.softmax(s_own, axis=-1).astype(v.dtype)
            out = jnp.einsum("bhqk,bhkd->bhqd", p, v_own)
        return out.transpose(0, 2, 1, 3)

    starts = jnp.arange(0, T, Q_CHUNK, dtype=jnp.int32)
    outs = lax.map(chunk, starts)
    return outs.transpose(1, 0, 2, 3, 4).reshape(B, T, H * D)


def setup_inputs(seed: int = 0) -> dict:
    key = jax.random.key(seed)
    ks = jax.random.split(key, 16)
    f32 = jnp.float32
    n = lambda k, shape, s: jax.random.normal(k, shape, f32) * s
    L = DEPTH
    return {
        "x": n(ks[0], (BATCH, SEQ, D_MODEL), 1.0),
        "c": n(ks[1], (BATCH, D_MODEL), 1.0),
        "w_ada": n(ks[2], (L, D_MODEL, N_MOD * D_MODEL), 0.5 * D_MODEL ** -0.5),
        "b_ada": n(ks[3], (L, N_MOD * D_MODEL), 0.02),
        "g_mix": 1.0 + n(ks[4], (L, D_MODEL), 0.02),
        "w_in": n(ks[5], (L, D_MODEL, IN_WIDTH), D_MODEL ** -0.5),
        "conv_w": n(ks[6], (L, CONV_KERNEL, CONV_WIDTH), CONV_KERNEL ** -0.5),
        "conv_b": n(ks[7], (L, CONV_WIDTH), 0.02),
        "ln_g": 1.0 + n(ks[8], (L, CONV_WIDTH), 0.02),
        "ln_b": n(ks[9], (L, CONV_WIDTH), 0.02),
        "w_out": n(ks[10], (L, D_MODEL, D_MODEL), D_MODEL ** -0.5),
        "g_ffn": 1.0 + n(ks[11], (L, D_MODEL), 0.02),
        "w_gate": n(ks[12], (L, D_MODEL, D_FF), D_MODEL ** -0.5),
        "w_up": n(ks[13], (L, D_MODEL, D_FF), D_MODEL ** -0.5),
        "w_down": n(ks[14], (L, D_FF, D_MODEL), D_FF ** -0.5),
        "g_final": 1.0 + n(ks[15], (D_MODEL,), 0.02),
    }


def reference(x, c, w_ada, b_ada, g_mix, w_in, conv_w, conv_b, ln_g, ln_b,
              w_out, g_ffn, w_gate, w_up, w_down, g_final):
    B, T, _ = x.shape
    pos = jnp.arange(T, dtype=jnp.float32)
    cs = jax.nn.silu(c)
    splits = [ATTN_WIDTH, 2 * ATTN_WIDTH, 3 * ATTN_WIDTH, 3 * ATTN_WIDTH + CONV_WIDTH]
    h = x
    for l in range(DEPTH):
        mod = cs @ w_ada[l] + b_ada[l]
        sh_m, sc_m, gt_m, sh_f, sc_f, gt_f = jnp.split(mod, N_MOD, axis=-1)

        u = modulate(rms_norm(h, g_mix[l]), sh_m, sc_m)
        proj = u @ w_in[l]
        q, k, v, ga, gb = jnp.split(proj, splits, axis=-1)
        to_heads = lambda t: t.reshape(B, T, ATTN_HEADS, HEAD_DIM).transpose(0, 2, 1, 3)
        q = rope(to_heads(q), pos)
        k = rope(to_heads(k), pos)
        v = to_heads(v)
        attn_out = moba_attention(q, k, v)
        conv_out = conv_module(ga, gb, conv_w[l], conv_b[l], ln_g[l], ln_b[l])
        mixed = jnp.concatenate([attn_out, conv_out], axis=-1) @ w_out[l]
        h = h + gt_m[:, None, :] * mixed

        u = modulate(rms_norm(h, g_ffn[l]), sh_f, sc_f)
        ff = (jax.nn.silu(u @ w_gate[l]) * (u @ w_up[l])) @ w_down[l]
        h = h + gt_f[:, None, :] * ff
    return rms_norm(h, g_final)
```

```python
import functools
import math

import jax
import jax.numpy as jnp
from jax import lax
from jax.experimental import pallas as pl
from jax.experimental.pallas import tpu as pltpu

ATTN_HEADS = 8
HEAD_DIM = 128
ATTN_WIDTH = ATTN_HEADS * HEAD_DIM
CONV_KERNEL = 31
MOBA_BLOCK = 256
MOBA_TOPK = 3
ROPE_THETA = 10000.0
N_MOD = 6
EPS = 1e-6

F32 = jnp.float32
BF16 = jnp.bfloat16

SUBLANES = 8
LANES = 128
MASK_BIG = 1e30
VMEM_LIMIT = 56 * 1024 * 1024

ADA_TN = 1024
ROW_TILE = 512
FF_TILE = 512
CONV_ROWS = 256
CONV_HALO = 32
CONV_CHUNK = 64

NT_DIMS = (((1,), (1,)), ((), ()))


def _params(*sem):
    return pltpu.CompilerParams(dimension_semantics=sem, vmem_limit_bytes=VMEM_LIMIT)


def _ada_kernel(c_ref, w_ref, b_ref, o_ref):
    c = c_ref[...]
    cs = (c * jax.nn.sigmoid(c)).astype(BF16)
    o_ref[...] = jnp.dot(cs, w_ref[...].astype(BF16), preferred_element_type=F32) + b_ref[...]


def _ada(c, w_ada, b_ada):
    B, D = c.shape
    N = w_ada.shape[1]
    return pl.pallas_call(
        _ada_kernel,
        grid=(N // ADA_TN,),
        in_specs=[
            pl.BlockSpec((B, D), lambda j: (0, 0)),
            pl.BlockSpec((D, ADA_TN), lambda j: (0, j)),
            pl.BlockSpec((1, ADA_TN), lambda j: (0, j)),
        ],
        out_specs=pl.BlockSpec((B, ADA_TN), lambda j: (0, j)),
        out_shape=jax.ShapeDtypeStruct((B, N), F32),
        compiler_params=_params("arbitrary"),
        name="ada",
    )(c, w_ada, b_ada.reshape(1, N))


def _norm_modulate(x, g, shift, scale):
    ms = jnp.mean(x * x, axis=-1, keepdims=True)
    y = x * lax.rsqrt(ms + EPS) * g
    return y * (1.0 + scale) + shift


def _inproj_kernel(x_ref, g_ref, sh_ref, sc_ref, w_ref, rope_ref, qkv_ref, glu_ref, u_scr):
    W = ATTN_WIDTH
    u_scr[...] = _norm_modulate(x_ref[0], g_ref[...], sh_ref[0], sc_ref[0]).astype(BF16)
    for part in range(2):
        p = jnp.dot(u_scr[...], w_ref[:, part * W:(part + 1) * W], preferred_element_type=F32)
        cos = rope_ref[:, (2 * part) * HEAD_DIM:(2 * part + 1) * HEAD_DIM]
        sin = rope_ref[:, (2 * part + 1) * HEAD_DIM:(2 * part + 2) * HEAD_DIM]
        for h in range(ATTN_HEADS):
            ph = p[:, h * HEAD_DIM:(h + 1) * HEAD_DIM]
            r = ph * cos + pltpu.roll(ph, HEAD_DIM // 2, axis=1) * sin
            qkv_ref[:, part * W + h * HEAD_DIM: part * W + (h + 1) * HEAD_DIM] = r.astype(BF16)
    v = jnp.dot(u_scr[...], w_ref[:, 2 * W:3 * W], preferred_element_type=F32)
    qkv_ref[:, 2 * W:3 * W] = v.astype(BF16)
    cw = (w_ref.shape[1] - 3 * W) // 2
    a = jnp.dot(u_scr[...], w_ref[:, 3 * W:3 * W + cw], preferred_element_type=F32)
    b = jnp.dot(u_scr[...], w_ref[:, 3 * W + cw:3 * W + 2 * cw], preferred_element_type=F32)
    glu_ref[0] = a * jax.nn.sigmoid(b)


def _inproj(x, g_mix, sh_m, sc_m, w_in_bf, rope_tab):
    B, T, D = x.shape
    NW = w_in_bf.shape[1]
    cw = (NW - 3 * ATTN_WIDTH) // 2
    tm = ROW_TILE
    nt = T // tm
    vec = pl.BlockSpec((1, 1, D), lambda b, i: (b, 0, 0))
    return pl.pallas_call(
        _inproj_kernel,
        grid=(B, nt),
        in_specs=[
            pl.BlockSpec((1, tm, D), lambda b, i: (b, i, 0)),
            pl.BlockSpec((1, D), lambda b, i: (0, 0)),
            vec, vec,
            pl.BlockSpec((D, NW), lambda b, i: (0, 0), pipeline_mode=pl.Buffered(1)),
            pl.BlockSpec((tm, 4 * HEAD_DIM), lambda b, i: (i, 0)),
        ],
        out_specs=[
            pl.BlockSpec((tm, 3 * ATTN_WIDTH), lambda b, i: (b * nt + i, 0)),
            pl.BlockSpec((1, tm, cw), lambda b, i: (b, i, 0)),
        ],
        out_shape=[
            jax.ShapeDtypeStruct((B * T, 3 * ATTN_WIDTH), BF16),
            jax.ShapeDtypeStruct((B, T, cw), F32),
        ],
        scratch_shapes=[pltpu.VMEM((tm, D), BF16)],
        compiler_params=_params("arbitrary", "arbitrary"),
        name="inproj",
    )(x, g_mix.reshape(1, D), sh_m, sc_m, w_in_bf, rope_tab)


def _attn_kernel(q_ref, k_ref, v_ref, o_ref, vt_scr, st_scr, pt_scr):
    T = q_ref.shape[0]
    BLK = MOBA_BLOCK
    nb = T // BLK
    ones_rows = vt_scr.shape[0] - HEAD_DIM

    vt_scr[0:HEAD_DIM, :] = v_ref[...].astype(F32).T.astype(BF16)
    vt_scr[HEAD_DIM:, :] = jnp.ones((ones_rows, T), BF16)

    kmean = jnp.mean(k_ref[...].astype(F32).reshape(nb, BLK, HEAD_DIM), axis=1)
    km_hi = kmean.astype(BF16).astype(F32)
    km2 = jnp.concatenate([km_hi, kmean - km_hi], axis=0).astype(BF16)
    g2 = lax.dot_general(km2, q_ref[...], NT_DIMS, preferred_element_type=F32)
    gate = g2[0:nb, :] + g2[nb:2 * nb, :]

    row = lax.broadcasted_iota(jnp.int32, (BLK, BLK), 0)
    col = lax.broadcasted_iota(jnp.int32, (BLK, BLK), 1)
    causal = row <= col
    blk_id = lax.broadcasted_iota(jnp.int32, (nb, BLK), 0)

    for i in range(nb):
        qi = q_ref[i * BLK:(i + 1) * BLK, :]
        topk = min(MOBA_TOPK, nb - 1)
        if i <= topk:
            bias = None
        else:
            g = gate[:, i * BLK:(i + 1) * BLK]
            rank = jnp.zeros((nb, BLK), jnp.int32)
            for n2 in range(i):
                gn = g[n2:n2 + 1, :]
                beats = (gn > g) | ((gn == g) & (n2 < blk_id))
                rank = rank + beats.astype(jnp.int32)
            bias = jnp.where(rank < topk, 0.0, -MASK_BIG).astype(F32)

        m = None
        for j in range(i + 1):
            s = lax.dot_general(k_ref[j * BLK:(j + 1) * BLK, :], qi, NT_DIMS, preferred_element_type=F32)
            if j == i:
                s = jnp.where(causal, s, -MASK_BIG)
            st_scr[j * BLK:(j + 1) * BLK, :] = s
            tmax = jnp.max(s, axis=0, keepdims=True)
            if bias is not None and j < i:
                tmax = tmax + bias[j:j + 1, :]
            m = tmax if m is None else jnp.maximum(m, tmax)

        for j in range(i + 1):
            c = m
            if bias is not None and j < i:
                c = m - bias[j:j + 1, :]
            p = jnp.exp2(st_scr[j * BLK:(j + 1) * BLK, :] - c)
            pt_scr[j * BLK:(j + 1) * BLK, :] = p.astype(BF16)

        kv = (i + 1) * BLK
        acc = jnp.dot(vt_scr[:, 0:kv], pt_scr[0:kv, :], preferred_element_type=F32)
        inv_l = 1.0 / acc[HEAD_DIM:HEAD_DIM + 1, :]
        o_ref[i * BLK:(i + 1) * BLK, :] = (acc[0:HEAD_DIM, :] * inv_l).T.astype(o_ref.dtype)


def _attn(qkv, B, T):
    H = ATTN_HEADS
    ones_rows = 16
    return pl.pallas_call(
        _attn_kernel,
        grid=(B, H),
        in_specs=[
            pl.BlockSpec((T, HEAD_DIM), lambda b, h: (b, h)),
            pl.BlockSpec((T, HEAD_DIM), lambda b, h: (b, H + h)),
            pl.BlockSpec((T, HEAD_DIM), lambda b, h: (b, 2 * H + h)),
        ],
        out_specs=pl.BlockSpec((T, HEAD_DIM), lambda b, h: (b, h)),
        out_shape=jax.ShapeDtypeStruct((B * T, ATTN_WIDTH), BF16),
        scratch_shapes=[
            pltpu.VMEM((HEAD_DIM + ones_rows, T), BF16),
            pltpu.VMEM((T, MOBA_BLOCK), F32),
            pltpu.VMEM((T, MOBA_BLOCK), BF16),
        ],
        compiler_params=_params("arbitrary", "arbitrary"),
        name="attn",
    )(qkv, qkv, qkv)


def _conv_kernel(main_ref, halo_ref, w_ref, cb_ref, lg_ref, lb_ref, o_ref, sh_scr, y_scr):
    tr = main_ref.shape[1]
    C = main_ref.shape[2]
    first = pl.program_id(1) == 0
    halo = jnp.where(first, 0.0, halo_ref[0])
    sh_scr[0, 0:CONV_HALO, :] = halo
    sh_scr[0, CONV_HALO:CONV_HALO + tr, :] = main_ref[0]
    ext = tr + CONV_HALO - SUBLANES
    for r in range(1, SUBLANES):
        sh_scr[r, 0:ext, :] = sh_scr[0, r:r + ext, :]

    lead = CONV_HALO - (CONV_KERNEL - 1)
    n_chunks = tr // CONV_CHUNK

    for cs in range(C // LANES):
        lanes = slice(cs * LANES, (cs + 1) * LANES)

        def chunk(ci, carry, lanes=lanes):
            base = pl.multiple_of(ci * CONV_CHUNK, CONV_CHUNK)
            acc = jnp.broadcast_to(cb_ref[:, lanes], (CONV_CHUNK, LANES))
            for j in range(CONV_KERNEL):
                s = lead + j
                tap = sh_scr[s % SUBLANES, pl.ds(base + (s // SUBLANES) * SUBLANES, CONV_CHUNK), lanes]
                acc = acc + tap * w_ref[j:j + 1, lanes]
            y_scr[pl.ds(base, CONV_CHUNK), lanes] = acc
            return carry

        lax.fori_loop(0, n_chunks, chunk, 0)

    y = y_scr[...]
    mu = jnp.mean(y, axis=-1, keepdims=True)
    d = y - mu
    var = jnp.mean(d * d, axis=-1, keepdims=True)
    yn = d * lax.rsqrt(var + EPS) * lg_ref[...] + lb_ref[...]
    o_ref[...] = (yn * jax.nn.sigmoid(yn)).astype(o_ref.dtype)


def _conv(glu, conv_w, conv_b, ln_g, ln_b):
    B, T, C = glu.shape
    tr = CONV_ROWS
    nt = T // tr
    hb = tr // CONV_HALO
    vec = pl.BlockSpec((1, C), lambda b, i: (0, 0))
    return pl.pallas_call(
        _conv_kernel,
        grid=(B, nt),
        in_specs=[
            pl.BlockSpec((1, tr, C), lambda b, i: (b, i, 0)),
            pl.BlockSpec((1, CONV_HALO, C), lambda b, i: (b, jnp.maximum(i * hb - 1, 0), 0)),
            pl.BlockSpec((CONV_KERNEL, C), lambda b, i: (0, 0)),
            vec, vec, vec,
        ],
        out_specs=pl.BlockSpec((tr, C), lambda b, i: (b * nt + i, 0)),
        out_shape=jax.ShapeDtypeStruct((B * T, C), BF16),
        scratch_shapes=[
            pltpu.VMEM((SUBLANES, tr + CONV_HALO, C), F32),
            pltpu.VMEM((tr, C), F32),
        ],
        compiler_params=_params("arbitrary", "arbitrary"),
        name="conv",
    )(glu, glu, conv_w, conv_b.reshape(1, C), ln_g.reshape(1, C), ln_b.reshape(1, C))


def _outproj_kernel(attn_ref, conv_ref, w_ref, x_ref, gt_ref, g_ref, sh_ref, sc_ref, h_ref, u_ref):
    aw = attn_ref.shape[1]
    mixed = (jnp.dot(attn_ref[...], w_ref[0:aw, :], preferred_element_type=F32)
             + jnp.dot(conv_ref[...], w_ref[aw:, :], preferred_element_type=F32))
    h = x_ref[0] + gt_ref[0] * mixed
    h_ref[...] = h
    u_ref[...] = _norm_modulate(h, g_ref[...], sh_ref[0], sc_ref[0]).astype(BF16)


def _outproj(attn, conv, w_out_bf, x, gt_m, g_ffn, sh_f, sc_f):
    B, T, D = x.shape
    aw, cw = attn.shape[1], conv.shape[1]
    tm = ROW_TILE
    nt = T // tm
    vec = pl.BlockSpec((1, 1, D), lambda b, i: (b, 0, 0))
    rows = lambda b, i: (b * nt + i, 0)
    return pl.pallas_call(
        _outproj_kernel,
        grid=(B, nt),
        in_specs=[
            pl.BlockSpec((tm, aw), rows),
            pl.BlockSpec((tm, cw), rows),
            pl.BlockSpec((D, D), lambda b, i: (0, 0), pipeline_mode=pl.Buffered(1)),
            pl.BlockSpec((1, tm, D), lambda b, i: (b, i, 0)),
            vec,
            pl.BlockSpec((1, D), lambda b, i: (0, 0)),
            vec, vec,
        ],
        out_specs=[pl.BlockSpec((tm, D), rows), pl.BlockSpec((tm, D), rows)],
        out_shape=[jax.ShapeDtypeStruct((B * T, D), F32), jax.ShapeDtypeStruct((B * T, D), BF16)],
        compiler_params=_params("arbitrary", "arbitrary"),
        name="outproj",
    )(attn, conv, w_out_bf, x, gt_m, g_ffn.reshape(1, D), sh_f, sc_f)


def _ffn_kernel(u_ref, wg_ref, wu_ref, wd_ref, h_ref, gt_ref, g_ref, o_ref):
    f = pl.program_id(2)
    u = u_ref[...]
    gte = jnp.dot(u, wg_ref[...], preferred_element_type=F32)
    up = jnp.dot(u, wu_ref[...], preferred_element_type=F32)
    a = (gte * jax.nn.sigmoid(gte) * up).astype(BF16)
    part = gt_ref[0] * jnp.dot(a, wd_ref[...], preferred_element_type=F32)

    @pl.when(f == 0)
    def _():
        o_ref[0] = h_ref[...] + part

    @pl.when(f > 0)
    def _():
        o_ref[0] += part

    @pl.when(f == pl.num_programs(2) - 1)
    def _():
        h = o_ref[0]
        ms = jnp.mean(h * h, axis=-1, keepdims=True)
        o_ref[0] = h * lax.rsqrt(ms + EPS) * g_ref[...]


def _ffn(u2, wg_bf, wu_bf, wd_bf, h1, gt_f, g_final, B, T):
    D = u2.shape[1]
    FF = wg_bf.shape[1]
    tm, tf = ROW_TILE, FF_TILE
    nt = T // tm
    rows = lambda b, i, f: (b * nt + i, 0)
    return pl.pallas_call(
        _ffn_kernel,
        grid=(B, nt, FF // tf),
        in_specs=[
            pl.BlockSpec((tm, D), rows),
            pl.BlockSpec((D, tf), lambda b, i, f: (0, f)),
            pl.BlockSpec((D, tf), lambda b, i, f: (0, f)),
            pl.BlockSpec((tf, D), lambda b, i, f: (f, 0)),
            pl.BlockSpec((tm, D), rows),
            pl.BlockSpec((1, 1, D), lambda b, i, f: (b, 0, 0)),
            pl.BlockSpec((1, D), lambda b, i, f: (0, 0)),
        ],
        out_specs=pl.BlockSpec((1, tm, D), lambda b, i, f: (b, i, 0)),
        out_shape=jax.ShapeDtypeStruct((B, T, D), F32),
        compiler_params=_params("arbitrary", "arbitrary", "arbitrary"),
        name="ffn",
    )(u2, wg_bf, wu_bf, wd_bf, h1, gt_f, g_final.reshape(1, D))


def _rope_tables(T):
    half = HEAD_DIM // 2
    inv = ROPE_THETA ** (-jnp.arange(half, dtype=F32) / half)
    ang = jnp.arange(T, dtype=F32)[:, None] * inv[None, :]
    cos, sin = jnp.cos(ang), jnp.sin(ang)
    cos_f = jnp.concatenate([cos, cos], axis=-1)
    sin_f = jnp.concatenate([-sin, sin], axis=-1)
    qs = (HEAD_DIM ** -0.5) * math.log2(math.e)
    return jnp.concatenate([cos_f * qs, sin_f * qs, cos_f, sin_f], axis=-1)


def kernel(x, c, w_ada, b_ada, g_mix, w_in, conv_w, conv_b, ln_g, ln_b, w_out, g_ffn, w_gate, w_up, w_down, g_final):
    B, T, D = x.shape
    depth = w_ada.shape[0]
    assert depth == 1 and T % MOBA_BLOCK == 0 and T % ROW_TILE == 0 and D % LANES == 0
    rope_tab = _rope_tables(T)
    h = x
    for l in range(depth):
        mod = _ada(c, w_ada[l], b_ada[l])
        sh_m, sc_m, gt_m, sh_f, sc_f, gt_f = [m.reshape(B, 1, D) for m in jnp.split(mod, N_MOD, axis=-1)]
        qkv, glu = _inproj(h, g_mix[l], sh_m, sc_m, w_in[l].astype(BF16), rope_tab)
        attn = _attn(qkv, B, T)
        conv = _conv(glu, conv_w[l], conv_b[l], ln_g[l], ln_b[l])
        h1, u2 = _outproj(attn, conv, w_out[l].astype(BF16), h, gt_m, g_ffn[l], sh_f, sc_f)
        out = _ffn(u2, w_gate[l].astype(BF16), w_up[l].astype(BF16), w_down[l].astype(BF16), h1, gt_f, g_final, B, T)
    return out
```

```python
import functools
import math

import jax
import jax.numpy as jnp
from jax import lax
from jax.experimental import pallas as pl
from jax.experimental.pallas import tpu as pltpu

ATTN_HEADS = 8
HEAD_DIM = 128
ATTN_WIDTH = ATTN_HEADS * HEAD_DIM
CONV_KERNEL = 31
MOBA_BLOCK = 256
MOBA_TOPK = 3
ROPE_THETA = 10000.0
N_MOD = 6
EPS = 1e-6

F32 = jnp.float32
BF16 = jnp.bfloat16

SUBLANES = 8
LANES = 128
MASK_BIG = 1e30
VMEM_LIMIT = 56 * 1024 * 1024

ADA_TN = 1024
ROW_TILE = 512
FF_TILE = 512
CONV_ROWS = 256
CONV_HALO = 32
CONV_CHUNK = 64

NT_DIMS = (((1,), (1,)), ((), ()))


def _params(*sem):
    return pltpu.CompilerParams(dimension_semantics=sem, vmem_limit_bytes=VMEM_LIMIT)


def _ada_kernel(c_ref, w_ref, b_ref, o_ref):
    c = c_ref[...]
    cs = (c * jax.nn.sigmoid(c)).astype(BF16)
    o_ref[...] = jnp.dot(cs, w_ref[...].astype(BF16), preferred_element_type=F32) + b_ref[...]


def _ada(c, w_ada, b_ada):
    B, D = c.shape
    N = w_ada.shape[1]
    return pl.pallas_call(
        _ada_kernel,
        grid=(N // ADA_TN,),
        in_specs=[
            pl.BlockSpec((B, D), lambda j: (0, 0)),
            pl.BlockSpec((D, ADA_TN), lambda j: (0, j)),
            pl.BlockSpec((1, ADA_TN), lambda j: (0, j)),
        ],
        out_specs=pl.BlockSpec((B, ADA_TN), lambda j: (0, j)),
        out_shape=jax.ShapeDtypeStruct((B, N), F32),
        compiler_params=_params("arbitrary"),
        name="ada",
    )(c, w_ada, b_ada.reshape(1, N))


def _norm_modulate(x, g, shift, scale):
    ms = jnp.mean(x * x, axis=-1, keepdims=True)
    y = x * lax.rsqrt(ms + EPS) * g
    return y * (1.0 + scale) + shift


def _inproj_kernel(x_ref, g_ref, sh_ref, sc_ref, w_ref, rope_ref, qkv_ref, glu_ref, u_scr):
    W = ATTN_WIDTH
    u_scr[...] = _norm_modulate(x_ref[0], g_ref[...], sh_ref[0], sc_ref[0]).astype(BF16)
    for part in range(2):
        p = jnp.dot(u_scr[...], w_ref[:, part * W:(part + 1) * W], preferred_element_type=F32)
        cos = rope_ref[:, (2 * part) * HEAD_DIM:(2 * part + 1) * HEAD_DIM]
        sin = rope_ref[:, (2 * part + 1) * HEAD_DIM:(2 * part + 2) * HEAD_DIM]
        for h in range(ATTN_HEADS):
            ph = p[:, h * HEAD_DIM:(h + 1) * HEAD_DIM]
            r = ph * cos + pltpu.roll(ph, HEAD_DIM // 2, axis=1) * sin
            qkv_ref[:, part * W + h * HEAD_DIM: part * W + (h + 1) * HEAD_DIM] = r.astype(BF16)
    v = jnp.dot(u_scr[...], w_ref[:, 2 * W:3 * W], preferred_element_type=F32)
    qkv_ref[:, 2 * W:3 * W] = v.astype(BF16)
    cw = (w_ref.shape[1] - 3 * W) // 2
    a = jnp.dot(u_scr[...], w_ref[:, 3 * W:3 * W + cw], preferred_element_type=F32)
    b = jnp.dot(u_scr[...], w_ref[:, 3 * W + cw:3 * W + 2 * cw], preferred_element_type=F32)
    glu_ref[0] = a * jax.nn.sigmoid(b)


def _inproj(x, g_mix, sh_m, sc_m, w_in_bf, rope_tab):
    B, T, D = x.shape
    NW = w_in_bf.shape[1]
    cw = (NW - 3 * ATTN_WIDTH) // 2
    tm = ROW_TILE
    nt = T // tm
    vec = pl.BlockSpec((1, 1, D), lambda b, i: (b, 0, 0))
    return pl.pallas_call(
        _inproj_kernel,
        grid=(B, nt),
        in_specs=[
            pl.BlockSpec((1, tm, D), lambda b, i: (b, i, 0)),
            pl.BlockSpec((1, D), lambda b, i: (0, 0)),
            vec, vec,
            pl.BlockSpec((D, NW), lambda b, i: (0, 0), pipeline_mode=pl.Buffered(1)),
            pl.BlockSpec((tm, 4 * HEAD_DIM), lambda b, i: (i, 0)),
        ],
        out_specs=[
            pl.BlockSpec((tm, 3 * ATTN_WIDTH), lambda b, i: (b * nt + i, 0)),
            pl.BlockSpec((1, tm, cw), lambda b, i: (b, i, 0)),
        ],
        out_shape=[
            jax.ShapeDtypeStruct((B * T, 3 * ATTN_WIDTH), BF16),
            jax.ShapeDtypeStruct((B, T, cw), F32),
        ],
        scratch_shapes=[pltpu.VMEM((tm, D), BF16)],
        compiler_params=_params("arbitrary", "arbitrary"),
        name="inproj",
    )(x, g_mix.reshape(1, D), sh_m, sc_m, w_in_bf, rope_tab)


def _attn_kernel(q_ref, k_ref, v_ref, o_ref, vt_scr, st_scr, pt_scr):
    T = q_ref.shape[0]
    BLK = MOBA_BLOCK
    nb = T // BLK
    ones_rows = vt_scr.shape[0] - HEAD_DIM

    vt_scr[0:HEAD_DIM, :] = v_ref[...].astype(F32).T.astype(BF16)
    vt_scr[HEAD_DIM:, :] = jnp.ones((ones_rows, T), BF16)

    kmean = jnp.mean(k_ref[...].astype(F32).reshape(nb, BLK, HEAD_DIM), axis=1)
    km_hi = kmean.astype(BF16).astype(F32)
    km2 = jnp.concatenate([km_hi, kmean - km_hi], axis=0).astype(BF16)
    g2 = lax.dot_general(km2, q_ref[...], NT_DIMS, preferred_element_type=F32)
    gate = g2[0:nb, :] + g2[nb:2 * nb, :]

    row = lax.broadcasted_iota(jnp.int32, (BLK, BLK), 0)
    col = lax.broadcasted_iota(jnp.int32, (BLK, BLK), 1)
    causal = row <= col
    blk_id = lax.broadcasted_iota(jnp.int32, (nb, BLK), 0)

    topk = min(MOBA_TOPK, nb - 1)

    def scores(i, slot):
        qi = q_ref[i * BLK:(i + 1) * BLK, :]
        if i <= topk:
            bias = None
        else:
            g = gate[:, i * BLK:(i + 1) * BLK]
            rank = jnp.zeros((nb, BLK), jnp.int32)
            for n2 in range(i):
                gn = g[n2:n2 + 1, :]
                beats = (gn > g) | ((gn == g) & (n2 < blk_id))
                rank = rank + beats.astype(jnp.int32)
            bias = jnp.where(rank < topk, 0.0, -MASK_BIG).astype(F32)
        m = None
        for j in range(i + 1):
            s = lax.dot_general(k_ref[j * BLK:(j + 1) * BLK, :], qi, NT_DIMS, preferred_element_type=F32)
            if j == i:
                s = jnp.where(causal, s, -MASK_BIG)
            st_scr[slot, j * BLK:(j + 1) * BLK, :] = s
            tmax = jnp.max(s, axis=0, keepdims=True)
            if bias is not None and j < i:
                tmax = tmax + bias[j:j + 1, :]
            m = tmax if m is None else jnp.maximum(m, tmax)
        return m, bias

    def probs(i, slot, m, bias):
        for j in range(i + 1):
            c = m
            if bias is not None and j < i:
                c = m - bias[j:j + 1, :]
            p = jnp.exp2(st_scr[slot, j * BLK:(j + 1) * BLK, :] - c)
            pt_scr[slot, j * BLK:(j + 1) * BLK, :] = p.astype(BF16)

    def output(i, slot):
        kv = (i + 1) * BLK
        acc = jnp.dot(vt_scr[:, 0:kv], pt_scr[slot, 0:kv, :], preferred_element_type=F32)
        inv_l = 1.0 / acc[HEAD_DIM:HEAD_DIM + 1, :]
        o_ref[i * BLK:(i + 1) * BLK, :] = (acc[0:HEAD_DIM, :] * inv_l).T.astype(o_ref.dtype)

    order = []
    lo, hi = 0, nb - 1
    while lo < hi:
        order.append((lo, hi))
        lo, hi = lo + 1, hi - 1
    if lo == hi:
        order.append((lo,))
    for group in order:
        stats = [scores(i, slot) for slot, i in enumerate(group)]
        for slot, i in enumerate(group):
            probs(i, slot, *stats[slot])
        for slot, i in enumerate(group):
            output(i, slot)


def _attn(qkv, B, T):
    H = ATTN_HEADS
    ones_rows = 16
    return pl.pallas_call(
        _attn_kernel,
        grid=(B, H),
        in_specs=[
            pl.BlockSpec((T, HEAD_DIM), lambda b, h: (b, h)),
            pl.BlockSpec((T, HEAD_DIM), lambda b, h: (b, H + h)),
            pl.BlockSpec((T, HEAD_DIM), lambda b, h: (b, 2 * H + h)),
        ],
        out_specs=pl.BlockSpec((T, HEAD_DIM), lambda b, h: (b, h)),
        out_shape=jax.ShapeDtypeStruct((B * T, ATTN_WIDTH), BF16),
        scratch_shapes=[
            pltpu.VMEM((HEAD_DIM + ones_rows, T), BF16),
            pltpu.VMEM((2, T, MOBA_BLOCK), F32),
            pltpu.VMEM((2, T, MOBA_BLOCK), BF16),
        ],
        compiler_params=_params("arbitrary", "arbitrary"),
        name="attn",
    )(qkv, qkv, qkv)


def _conv_kernel(main_ref, halo_ref, w_ref, cb_ref, lg_ref, lb_ref, o_ref, sh_scr, y_scr):
    tr = main_ref.shape[1]
    C = main_ref.shape[2]
    first = pl.program_id(1) == 0
    ext = tr + CONV_HALO - SUBLANES
    lead = CONV_HALO - (CONV_KERNEL - 1)
    n_chunks = tr // CONV_CHUNK

    for cs in range(C // LANES):
        lanes = slice(cs * LANES, (cs + 1) * LANES)
        sh_scr[0, cs, 0:CONV_HALO, :] = jnp.where(first, 0.0, halo_ref[0, :, lanes])
        sh_scr[0, cs, CONV_HALO:CONV_HALO + tr, :] = main_ref[0, :, lanes]
        for r in range(1, SUBLANES):
            sh_scr[r, cs, 0:ext, :] = sh_scr[0, cs, r:r + ext, :]

        taps_w = [jnp.broadcast_to(w_ref[j:j + 1, lanes], (CONV_CHUNK, LANES)) for j in range(CONV_KERNEL)]
        bias = jnp.broadcast_to(cb_ref[:, lanes], (CONV_CHUNK, LANES))

        def chunk(ci, carry, cs=cs, lanes=lanes, taps_w=taps_w, bias=bias):
            base = pl.multiple_of(ci * CONV_CHUNK, CONV_CHUNK)
            acc = bias
            for j in range(CONV_KERNEL):
                s = lead + j
                tap = sh_scr[s % SUBLANES, cs, pl.ds(base + (s // SUBLANES) * SUBLANES, CONV_CHUNK), :]
                acc = acc + tap * taps_w[j]
            y_scr[pl.ds(base, CONV_CHUNK), lanes] = acc
            return carry

        lax.fori_loop(0, n_chunks, chunk, 0)

    y = y_scr[...]
    mu = jnp.mean(y, axis=-1, keepdims=True)
    d = y - mu
    var = jnp.mean(d * d, axis=-1, keepdims=True)
    yn = d * lax.rsqrt(var + EPS) * lg_ref[...] + lb_ref[...]
    o_ref[...] = (yn * jax.nn.sigmoid(yn)).astype(o_ref.dtype)


def _conv(glu, conv_w, conv_b, ln_g, ln_b):
    B, T, C = glu.shape
    tr = CONV_ROWS
    nt = T // tr
    hb = tr // CONV_HALO
    vec = pl.BlockSpec((1, C), lambda b, i: (0, 0))
    return pl.pallas_call(
        _conv_kernel,
        grid=(B, nt),
        in_specs=[
            pl.BlockSpec((1, tr, C), lambda b, i: (b, i, 0)),
            pl.BlockSpec((1, CONV_HALO, C), lambda b, i: (b, jnp.maximum(i * hb - 1, 0), 0)),
            pl.BlockSpec((CONV_KERNEL, C), lambda b, i: (0, 0)),
            vec, vec, vec,
        ],
        out_specs=pl.BlockSpec((tr, C), lambda b, i: (b * nt + i, 0)),
        out_shape=jax.ShapeDtypeStruct((B * T, C), BF16),
        scratch_shapes=[
            pltpu.VMEM((SUBLANES, C // LANES, tr + CONV_HALO, LANES), F32),
            pltpu.VMEM((tr, C), F32),
        ],
        compiler_params=_params("arbitrary", "arbitrary"),
        name="conv",
    )(glu, glu, conv_w, conv_b.reshape(1, C), ln_g.reshape(1, C), ln_b.reshape(1, C))


def _outproj_kernel(attn_ref, conv_ref, w_ref, x_ref, gt_ref, g_ref, sh_ref, sc_ref, h_ref, u_ref):
    aw = attn_ref.shape[1]
    mixed = (jnp.dot(attn_ref[...], w_ref[0:aw, :], preferred_element_type=F32)
             + jnp.dot(conv_ref[...], w_ref[aw:, :], preferred_element_type=F32))
    h = x_ref[0] + gt_ref[0] * mixed
    h_ref[...] = h
    u_ref[...] = _norm_modulate(h, g_ref[...], sh_ref[0], sc_ref[0]).astype(BF16)


def _outproj(attn, conv, w_out_bf, x, gt_m, g_ffn, sh_f, sc_f):
    B, T, D = x.shape
    aw, cw = attn.shape[1], conv.shape[1]
    tm = ROW_TILE
    nt = T // tm
    vec = pl.BlockSpec((1, 1, D), lambda b, i: (b, 0, 0))
    rows = lambda b, i: (b * nt + i, 0)
    return pl.pallas_call(
        _outproj_kernel,
        grid=(B, nt),
        in_specs=[
            pl.BlockSpec((tm, aw), rows),
            pl.BlockSpec((tm, cw), rows),
            pl.BlockSpec((D, D), lambda b, i: (0, 0), pipeline_mode=pl.Buffered(1)),
            pl.BlockSpec((1, tm, D), lambda b, i: (b, i, 0)),
            vec,
            pl.BlockSpec((1, D), lambda b, i: (0, 0)),
            vec, vec,
        ],
        out_specs=[pl.BlockSpec((tm, D), rows), pl.BlockSpec((tm, D), rows)],
        out_shape=[jax.ShapeDtypeStruct((B * T, D), F32), jax.ShapeDtypeStruct((B * T, D), BF16)],
        compiler_params=_params("arbitrary", "arbitrary"),
        name="outproj",
    )(attn, conv, w_out_bf, x, gt_m, g_ffn.reshape(1, D), sh_f, sc_f)


def _ffn_kernel(u_ref, wg_ref, wu_ref, wd_ref, h_ref, gt_ref, g_ref, o_ref):
    f = pl.program_id(2)

    @pl.when(f == 0)
    def _():
        o_ref[0] = h_ref[...]

    u = u_ref[...]
    gte = jnp.dot(u, wg_ref[...], preferred_element_type=F32)
    up = jnp.dot(u, wu_ref[...], preferred_element_type=F32)
    a = (gte * jax.nn.sigmoid(gte) * up).astype(BF16)
    o_ref[0] += gt_ref[0] * jnp.dot(a, wd_ref[...], preferred_element_type=F32)

    @pl.when(f == pl.num_programs(2) - 1)
    def _():
        h = o_ref[0]
        ms = jnp.mean(h * h, axis=-1, keepdims=True)
        o_ref[0] = h * lax.rsqrt(ms + EPS) * g_ref[...]


def _ffn(u2, wg_bf, wu_bf, wd_bf, h1, gt_f, g_final, B, T):
    D = u2.shape[1]
    FF = wg_bf.shape[1]
    tm, tf = ROW_TILE, FF_TILE
    nt = T // tm
    rows = lambda b, i, f: (b * nt + i, 0)
    return pl.pallas_call(
        _ffn_kernel,
        grid=(B, nt, FF // tf),
        in_specs=[
            pl.BlockSpec((tm, D), rows),
            pl.BlockSpec((D, tf), lambda b, i, f: (0, f)),
            pl.BlockSpec((D, tf), lambda b, i, f: (0, f)),
            pl.BlockSpec((tf, D), lambda b, i, f: (f, 0)),
            pl.BlockSpec((tm, D), rows),
            pl.BlockSpec((1, 1, D), lambda b, i, f: (b, 0, 0)),
            pl.BlockSpec((1, D), lambda b, i, f: (0, 0)),
        ],
        out_specs=pl.BlockSpec((1, tm, D), lambda b, i, f: (b, i, 0)),
        out_shape=jax.ShapeDtypeStruct((B, T, D), F32),
        compiler_params=_params("arbitrary", "arbitrary", "arbitrary"),
        name="ffn",
    )(u2, wg_bf, wu_bf, wd_bf, h1, gt_f, g_final.reshape(1, D))


def _rope_tables(T):
    half = HEAD_DIM // 2
    inv = ROPE_THETA ** (-jnp.arange(half, dtype=F32) / half)
    ang = jnp.arange(T, dtype=F32)[:, None] * inv[None, :]
    cos, sin = jnp.cos(ang), jnp.sin(ang)
    cos_f = jnp.concatenate([cos, cos], axis=-1)
    sin_f = jnp.concatenate([-sin, sin], axis=-1)
    qs = (HEAD_DIM ** -0.5) * math.log2(math.e)
    return jnp.concatenate([cos_f * qs, sin_f * qs, cos_f, sin_f], axis=-1)


def kernel(x, c, w_ada, b_ada, g_mix, w_in, conv_w, conv_b, ln_g, ln_b, w_out, g_ffn, w_gate, w_up, w_down, g_final):
    B, T, D = x.shape
    depth = w_ada.shape[0]
    assert depth == 1 and T % MOBA_BLOCK == 0 and T % ROW_TILE == 0 and D % LANES == 0
    rope_tab = _rope_tables(T)
    h = x
    for l in range(depth):
        mod = _ada(c, w_ada[l], b_ada[l])
        sh_m, sc_m, gt_m, sh_f, sc_f, gt_f = [m.reshape(B, 1, D) for m in jnp.split(mod, N_MOD, axis=-1)]
        qkv, glu = _inproj(h, g_mix[l], sh_m, sc_m, w_in[l].astype(BF16), rope_tab)
        attn = _attn(qkv, B, T)
        conv = _conv(glu, conv_w[l], conv_b[l], ln_g[l], ln_b[l])
        h1, u2 = _outproj(attn, conv, w_out[l].astype(BF16), h, gt_m, g_ffn[l], sh_f, sc_f)
        out = _ffn(u2, w_gate[l].astype(BF16), w_up[l].astype(BF16), w_down[l].astype(BF16), h1, gt_f, g_final, B, T)
    return out
```

```python
import functools
import math

import jax
import jax.numpy as jnp
from jax import lax
from jax.experimental import pallas as pl
from jax.experimental.pallas import tpu as pltpu

ATTN_HEADS = 8
HEAD_DIM = 128
ATTN_WIDTH = ATTN_HEADS * HEAD_DIM
CONV_KERNEL = 31
MOBA_BLOCK = 256
MOBA_TOPK = 3
ROPE_THETA = 10000.0
N_MOD = 6
EPS = 1e-6

F32 = jnp.float32
BF16 = jnp.bfloat16

SUBLANES = 8
LANES = 128
MASK_BIG = 1e30
VMEM_LIMIT = 56 * 1024 * 1024
FFN_VMEM_LIMIT = 60 * 1024 * 1024

ADA_TN = 1024
ROW_TILE = 512
FFN_ROW_TILE = 1024
FFN_DOWN_COLS = 512
FF_TILE = 512
ATTN_HEADS_PER_STEP = 4
CONV_ROWS = 256
CONV_HALO = 32
CONV_CHUNK = 64

NT_DIMS = (((1,), (1,)), ((), ()))


def _params(*sem, vmem_limit=VMEM_LIMIT):
    return pltpu.CompilerParams(dimension_semantics=sem, vmem_limit_bytes=vmem_limit)


def _ada_kernel(c_ref, w_ref, b_ref, o_ref):
    c = c_ref[...]
    cs = (c * jax.nn.sigmoid(c)).astype(BF16)
    o_ref[...] = jnp.dot(cs, w_ref[...].astype(BF16), preferred_element_type=F32) + b_ref[...]


def _ada(c, w_ada, b_ada):
    B, D = c.shape
    N = w_ada.shape[1]
    return pl.pallas_call(
        _ada_kernel,
        grid=(N // ADA_TN,),
        in_specs=[
            pl.BlockSpec((B, D), lambda j: (0, 0)),
            pl.BlockSpec((D, ADA_TN), lambda j: (0, j)),
            pl.BlockSpec((1, ADA_TN), lambda j: (0, j)),
        ],
        out_specs=pl.BlockSpec((B, ADA_TN), lambda j: (0, j)),
        out_shape=jax.ShapeDtypeStruct((B, N), F32),
        compiler_params=_params("arbitrary"),
        name="ada",
    )(c, w_ada, b_ada.reshape(1, N))


def _norm_modulate(x, g, shift, scale):
    ms = jnp.mean(x * x, axis=-1, keepdims=True)
    y = x * lax.rsqrt(ms + EPS) * g
    return y * (1.0 + scale) + shift


def _inproj_kernel(x_ref, g_ref, sh_ref, sc_ref, w_ref, rope_ref, qkv_ref, glu_ref, u_scr):
    W = ATTN_WIDTH
    u_scr[...] = _norm_modulate(x_ref[0], g_ref[...], sh_ref[0], sc_ref[0]).astype(BF16)
    for part in range(2):
        p = jnp.dot(u_scr[...], w_ref[:, part * W:(part + 1) * W], preferred_element_type=F32)
        cos = rope_ref[:, (2 * part) * HEAD_DIM:(2 * part + 1) * HEAD_DIM]
        sin = rope_ref[:, (2 * part + 1) * HEAD_DIM:(2 * part + 2) * HEAD_DIM]
        for h in range(ATTN_HEADS):
            ph = p[:, h * HEAD_DIM:(h + 1) * HEAD_DIM]
            r = ph * cos + pltpu.roll(ph, HEAD_DIM // 2, axis=1) * sin
            qkv_ref[:, part * W + h * HEAD_DIM: part * W + (h + 1) * HEAD_DIM] = r.astype(BF16)
    v = jnp.dot(u_scr[...], w_ref[:, 2 * W:3 * W], preferred_element_type=F32)
    qkv_ref[:, 2 * W:3 * W] = v.astype(BF16)
    cw = (w_ref.shape[1] - 3 * W) // 2
    a = jnp.dot(u_scr[...], w_ref[:, 3 * W:3 * W + cw], preferred_element_type=F32)
    b = jnp.dot(u_scr[...], w_ref[:, 3 * W + cw:3 * W + 2 * cw], preferred_element_type=F32)
    glu_ref[0] = a * jax.nn.sigmoid(b)


def _inproj(x, g_mix, sh_m, sc_m, w_in_bf, rope_tab):
    B, T, D = x.shape
    NW = w_in_bf.shape[1]
    cw = (NW - 3 * ATTN_WIDTH) // 2
    tm = ROW_TILE
    nt = T // tm
    vec = pl.BlockSpec((1, 1, D), lambda b, i: (b, 0, 0))
    return pl.pallas_call(
        _inproj_kernel,
        grid=(B, nt),
        in_specs=[
            pl.BlockSpec((1, tm, D), lambda b, i: (b, i, 0)),
            pl.BlockSpec((1, D), lambda b, i: (0, 0)),
            vec, vec,
            pl.BlockSpec((D, NW), lambda b, i: (0, 0), pipeline_mode=pl.Buffered(1)),
            pl.BlockSpec((tm, 4 * HEAD_DIM), lambda b, i: (i, 0)),
        ],
        out_specs=[
            pl.BlockSpec((tm, 3 * ATTN_WIDTH), lambda b, i: (b * nt + i, 0)),
            pl.BlockSpec((1, tm, cw), lambda b, i: (b, i, 0)),
        ],
        out_shape=[
            jax.ShapeDtypeStruct((B * T, 3 * ATTN_WIDTH), BF16),
            jax.ShapeDtypeStruct((B, T, cw), F32),
        ],
        scratch_shapes=[pltpu.VMEM((tm, D), BF16)],
        compiler_params=_params("arbitrary", "arbitrary"),
        name="inproj",
    )(x, g_mix.reshape(1, D), sh_m, sc_m, w_in_bf, rope_tab)


def _attn_kernel(q_ref, k_ref, v_ref, o_ref, vt_scr, st_scr, pt_scr):
    T = q_ref.shape[0]
    BLK = MOBA_BLOCK
    nb = T // BLK
    nh = q_ref.shape[1] // HEAD_DIM
    ones_rows = vt_scr.shape[1] - HEAD_DIM
    topk = min(MOBA_TOPK, nb - 1)

    row = lax.broadcasted_iota(jnp.int32, (BLK, BLK), 0)
    col = lax.broadcasted_iota(jnp.int32, (BLK, BLK), 1)
    causal = row <= col
    blk_id = lax.broadcasted_iota(jnp.int32, (nb, BLK), 0)

    def prologue(h):
        hd = slice(h * HEAD_DIM, (h + 1) * HEAD_DIM)
        vt_scr[h, 0:HEAD_DIM, :] = v_ref[:, hd].astype(F32).T.astype(BF16)
        vt_scr[h, HEAD_DIM:, :] = jnp.ones((ones_rows, T), BF16)
        kmean = jnp.mean(k_ref[:, hd].astype(F32).reshape(nb, BLK, HEAD_DIM), axis=1)
        km_hi = kmean.astype(BF16).astype(F32)
        km2 = jnp.concatenate([km_hi, kmean - km_hi], axis=0).astype(BF16)
        g2 = lax.dot_general(km2, q_ref[:, hd], NT_DIMS, preferred_element_type=F32)
        return g2[0:nb, :] + g2[nb:2 * nb, :]

    def scores(h, gate, i, slot):
        hd = slice(h * HEAD_DIM, (h + 1) * HEAD_DIM)
        qi = q_ref[i * BLK:(i + 1) * BLK, hd]
        if i <= topk:
            bias = None
        else:
            g = gate[:, i * BLK:(i + 1) * BLK]
            rank = jnp.zeros((nb, BLK), jnp.int32)
            for n2 in range(i):
                gn = g[n2:n2 + 1, :]
                beats = (gn > g) | ((gn == g) & (n2 < blk_id))
                rank = rank + beats.astype(jnp.int32)
            bias = jnp.where(rank < topk, 0.0, -MASK_BIG).astype(F32)
        m = None
        for j in range(i + 1):
            s = lax.dot_general(k_ref[j * BLK:(j + 1) * BLK, hd], qi, NT_DIMS, preferred_element_type=F32)
            if j == i:
                s = jnp.where(causal, s, -MASK_BIG)
            st_scr[slot, j * BLK:(j + 1) * BLK, :] = s
            tmax = jnp.max(s, axis=0, keepdims=True)
            if bias is not None and j < i:
                tmax = tmax + bias[j:j + 1, :]
            m = tmax if m is None else jnp.maximum(m, tmax)
        return m, bias

    def probs(i, slot, m, bias):
        for j in range(i + 1):
            c = m
            if bias is not None and j < i:
                c = m - bias[j:j + 1, :]
            p = jnp.exp2(st_scr[slot, j * BLK:(j + 1) * BLK, :] - c)
            pt_scr[slot, j * BLK:(j + 1) * BLK, :] = p.astype(BF16)

    def output(h, i, slot):
        kv = (i + 1) * BLK
        acc = jnp.dot(vt_scr[h, :, 0:kv], pt_scr[slot, 0:kv, :], preferred_element_type=F32)
        inv_l = 1.0 / acc[HEAD_DIM:HEAD_DIM + 1, :]
        o_ref[i * BLK:(i + 1) * BLK, h * HEAD_DIM:(h + 1) * HEAD_DIM] = (acc[0:HEAD_DIM, :] * inv_l).T.astype(o_ref.dtype)

    pairs = []
    lo, hi = 0, nb - 1
    while lo < hi:
        pairs.append((lo, hi))
        lo, hi = lo + 1, hi - 1
    if lo == hi:
        pairs.append((lo,))
    gates = [prologue(h) for h in range(nh)]
    for group in pairs:
        chains = [(h, i) for h in range(nh) for i in group]
        stats = [scores(h, gates[h], i, slot) for slot, (h, i) in enumerate(chains)]
        for slot, (h, i) in enumerate(chains):
            probs(i, slot, *stats[slot])
        for slot, (h, i) in enumerate(chains):
            output(h, i, slot)


def _attn(qkv, B, T):
    H = ATTN_HEADS
    nh = ATTN_HEADS_PER_STEP
    hw = nh * HEAD_DIM
    ones_rows = 16
    return pl.pallas_call(
        _attn_kernel,
        grid=(B, H // nh),
        in_specs=[
            pl.BlockSpec((T, hw), lambda b, h: (b, h)),
            pl.BlockSpec((T, hw), lambda b, h: (b, H // nh + h)),
            pl.BlockSpec((T, hw), lambda b, h: (b, 2 * (H // nh) + h)),
        ],
        out_specs=pl.BlockSpec((T, hw), lambda b, h: (b, h)),
        out_shape=jax.ShapeDtypeStruct((B * T, ATTN_WIDTH), BF16),
        scratch_shapes=[
            pltpu.VMEM((nh, HEAD_DIM + ones_rows, T), BF16),
            pltpu.VMEM((2 * nh, T, MOBA_BLOCK), F32),
            pltpu.VMEM((2 * nh, T, MOBA_BLOCK), BF16),
        ],
        compiler_params=_params("arbitrary", "arbitrary"),
        name="attn",
    )(qkv, qkv, qkv)


def _conv_kernel(main_ref, halo_ref, w_ref, cb_ref, lg_ref, lb_ref, o_ref, sh_scr, y_scr):
    tr = main_ref.shape[1]
    C = main_ref.shape[2]
    first = pl.program_id(1) == 0
    ext = tr + CONV_HALO - SUBLANES
    lead = CONV_HALO - (CONV_KERNEL - 1)
    n_chunks = tr // CONV_CHUNK

    for cs in range(C // LANES):
        lanes = slice(cs * LANES, (cs + 1) * LANES)
        sh_scr[0, cs, 0:CONV_HALO, :] = jnp.where(first, 0.0, halo_ref[0, :, lanes])
        sh_scr[0, cs, CONV_HALO:CONV_HALO + tr, :] = main_ref[0, :, lanes]
        for r in range(1, SUBLANES):
            sh_scr[r, cs, 0:ext, :] = sh_scr[0, cs, r:r + ext, :]

        taps_w = [jnp.broadcast_to(w_ref[j:j + 1, lanes], (CONV_CHUNK, LANES)) for j in range(CONV_KERNEL)]
        bias = jnp.broadcast_to(cb_ref[:, lanes], (CONV_CHUNK, LANES))

        def chunk(ci, carry, cs=cs, lanes=lanes, taps_w=taps_w, bias=bias):
            base = pl.multiple_of(ci * CONV_CHUNK, CONV_CHUNK)
            acc = bias
            for j in range(CONV_KERNEL):
                s = lead + j
                tap = sh_scr[s % SUBLANES, cs, pl.ds(base + (s // SUBLANES) * SUBLANES, CONV_CHUNK), :]
                acc = acc + tap * taps_w[j]
            y_scr[pl.ds(base, CONV_CHUNK), lanes] = acc
            return carry

        lax.fori_loop(0, n_chunks, chunk, 0)

    y = y_scr[...]
    mu = jnp.mean(y, axis=-1, keepdims=True)
    d = y - mu
    var = jnp.mean(d * d, axis=-1, keepdims=True)
    yn = d * lax.rsqrt(var + EPS) * lg_ref[...] + lb_ref[...]
    o_ref[...] = (yn * jax.nn.sigmoid(yn)).astype(o_ref.dtype)


def _conv(glu, conv_w, conv_b, ln_g, ln_b):
    B, T, C = glu.shape
    tr = CONV_ROWS
    nt = T // tr
    hb = tr // CONV_HALO
    vec = pl.BlockSpec((1, C), lambda b, i: (0, 0))
    return pl.pallas_call(
        _conv_kernel,
        grid=(B, nt),
        in_specs=[
            pl.BlockSpec((1, tr, C), lambda b, i: (b, i, 0)),
            pl.BlockSpec((1, CONV_HALO, C), lambda b, i: (b, jnp.maximum(i * hb - 1, 0), 0)),
            pl.BlockSpec((CONV_KERNEL, C), lambda b, i: (0, 0)),
            vec, vec, vec,
        ],
        out_specs=pl.BlockSpec((tr, C), lambda b, i: (b * nt + i, 0)),
        out_shape=jax.ShapeDtypeStruct((B * T, C), BF16),
        scratch_shapes=[
            pltpu.VMEM((SUBLANES, C // LANES, tr + CONV_HALO, LANES), F32),
            pltpu.VMEM((tr, C), F32),
        ],
        compiler_params=_params("arbitrary", "arbitrary"),
        name="conv",
    )(glu, glu, conv_w, conv_b.reshape(1, C), ln_g.reshape(1, C), ln_b.reshape(1, C))


def _outproj_kernel(attn_ref, conv_ref, w_ref, x_ref, gt_ref, g_ref, sh_ref, sc_ref, h_ref, u_ref):
    aw = attn_ref.shape[1]
    mixed = (jnp.dot(attn_ref[...], w_ref[0:aw, :], preferred_element_type=F32)
             + jnp.dot(conv_ref[...], w_ref[aw:, :], preferred_element_type=F32))
    h = x_ref[0] + gt_ref[0] * mixed
    h_ref[...] = h
    u_ref[...] = _norm_modulate(h, g_ref[...], sh_ref[0], sc_ref[0]).astype(BF16)


def _outproj(attn, conv, w_out_bf, x, gt_m, g_ffn, sh_f, sc_f):
    B, T, D = x.shape
    aw, cw = attn.shape[1], conv.shape[1]
    tm = ROW_TILE
    nt = T // tm
    vec = pl.BlockSpec((1, 1, D), lambda b, i: (b, 0, 0))
    rows = lambda b, i: (b * nt + i, 0)
    return pl.pallas_call(
        _outproj_kernel,
        grid=(B, nt),
        in_specs=[
            pl.BlockSpec((tm, aw), rows),
            pl.BlockSpec((tm, cw), rows),
            pl.BlockSpec((D, D), lambda b, i: (0, 0), pipeline_mode=pl.Buffered(1)),
            pl.BlockSpec((1, tm, D), lambda b, i: (b, i, 0)),
            vec,
            pl.BlockSpec((1, D), lambda b, i: (0, 0)),
            vec, vec,
        ],
        out_specs=[pl.BlockSpec((tm, D), rows), pl.BlockSpec((tm, D), rows)],
        out_shape=[jax.ShapeDtypeStruct((B * T, D), F32), jax.ShapeDtypeStruct((B * T, D), BF16)],
        compiler_params=_params("arbitrary", "arbitrary"),
        name="outproj",
    )(attn, conv, w_out_bf, x, gt_m, g_ffn.reshape(1, D), sh_f, sc_f)


def _ffn_kernel(u_ref, wg_ref, wu_ref, wd_ref, h_hbm, gt_ref, g_ref, o_ref, h_scr, h_sem):
    b, i, f = pl.program_id(0), pl.program_id(1), pl.program_id(2)
    tm = u_ref.shape[0]
    nf = pl.num_programs(2)
    row0 = pl.multiple_of((b * pl.num_programs(1) + i) * tm, tm)
    h_copy = pltpu.make_async_copy(h_hbm.at[pl.ds(row0, tm), :], h_scr, h_sem.at[0])

    @pl.when(f == 0)
    def _():
        h_copy.start()
        o_ref[0] = jnp.zeros(o_ref.shape[1:], F32)

    u = u_ref[...]
    gte = jnp.dot(u, wg_ref[...], preferred_element_type=F32)
    up = jnp.dot(u, wu_ref[...], preferred_element_type=F32)
    a = (gte * jax.nn.sigmoid(gte) * up).astype(BF16)
    D = o_ref.shape[2]
    for n0 in range(0, D, FFN_DOWN_COLS):
        cols = slice(n0, n0 + FFN_DOWN_COLS)
        o_ref[0, :, cols] += jnp.dot(a, wd_ref[:, cols], preferred_element_type=F32)

    @pl.when(f == nf - 1)
    def _():
        h_copy.wait()
        h = h_scr[...] + gt_ref[0] * o_ref[0]
        ms = jnp.mean(h * h, axis=-1, keepdims=True)
        o_ref[0] = h * lax.rsqrt(ms + EPS) * g_ref[...]


def _ffn(u2, wg_bf, wu_bf, wd_bf, h1, gt_f, g_final, B, T):
    D = u2.shape[1]
    FF = wg_bf.shape[1]
    tm, tf = FFN_ROW_TILE, FF_TILE
    nt = T // tm
    rows = lambda b, i, f: (b * nt + i, 0)
    return pl.pallas_call(
        _ffn_kernel,
        grid=(B, nt, FF // tf),
        in_specs=[
            pl.BlockSpec((tm, D), rows),
            pl.BlockSpec((D, tf), lambda b, i, f: (0, f)),
            pl.BlockSpec((D, tf), lambda b, i, f: (0, f)),
            pl.BlockSpec((tf, D), lambda b, i, f: (f, 0)),
            pl.BlockSpec(memory_space=pl.ANY),
            pl.BlockSpec((1, 1, D), lambda b, i, f: (b, 0, 0)),
            pl.BlockSpec((1, D), lambda b, i, f: (0, 0)),
        ],
        out_specs=pl.BlockSpec((1, tm, D), lambda b, i, f: (b, i, 0)),
        out_shape=jax.ShapeDtypeStruct((B, T, D), F32),
        scratch_shapes=[pltpu.VMEM((tm, D), F32), pltpu.SemaphoreType.DMA((1,))],
        compiler_params=_params("arbitrary", "arbitrary", "arbitrary", vmem_limit=FFN_VMEM_LIMIT),
        name="ffn",
    )(u2, wg_bf, wu_bf, wd_bf, h1, gt_f, g_final.reshape(1, D))


def _rope_tables(T):
    half = HEAD_DIM // 2
    inv = ROPE_THETA ** (-jnp.arange(half, dtype=F32) / half)
    ang = jnp.arange(T, dtype=F32)[:, None] * inv[None, :]
    cos, sin = jnp.cos(ang), jnp.sin(ang)
    cos_f = jnp.concatenate([cos, cos], axis=-1)
    sin_f = jnp.concatenate([-sin, sin], axis=-1)
    qs = (HEAD_DIM ** -0.5) * math.log2(math.e)
    return jnp.concatenate([cos_f * qs, sin_f * qs, cos_f, sin_f], axis=-1)


def kernel(x, c, w_ada, b_ada, g_mix, w_in, conv_w, conv_b, ln_g, ln_b, w_out, g_ffn, w_gate, w_up, w_down, g_final):
    B, T, D = x.shape
    depth = w_ada.shape[0]
    assert depth == 1 and T % MOBA_BLOCK == 0 and T % ROW_TILE == 0 and T % FFN_ROW_TILE == 0 and D % LANES == 0
    rope_tab = _rope_tables(T)
    h = x
    for l in range(depth):
        mod = _ada(c, w_ada[l], b_ada[l])
        sh_m, sc_m, gt_m, sh_f, sc_f, gt_f = [m.reshape(B, 1, D) for m in jnp.split(mod, N_MOD, axis=-1)]
        qkv, glu = _inproj(h, g_mix[l], sh_m, sc_m, w_in[l].astype(BF16), rope_tab)
        attn = _attn(qkv, B, T)
        conv = _conv(glu, conv_w[l], conv_b[l], ln_g[l], ln_b[l])
        h1, u2 = _outproj(attn, conv, w_out[l].astype(BF16), h, gt_m, g_ffn[l], sh_f, sc_f)
        out = _ffn(u2, w_gate[l].astype(BF16), w_up[l].astype(BF16), w_down[l].astype(BF16), h1, gt_f, g_final, B, T)
    return out
```

```python
import functools
import math

import jax
import jax.numpy as jnp
from jax import lax
from jax.experimental import pallas as pl
from jax.experimental.pallas import tpu as pltpu

ATTN_HEADS = 8
HEAD_DIM = 128
ATTN_WIDTH = ATTN_HEADS * HEAD_DIM
CONV_KERNEL = 31
MOBA_BLOCK = 256
MOBA_TOPK = 3
ROPE_THETA = 10000.0
N_MOD = 6
EPS = 1e-6

F32 = jnp.float32
BF16 = jnp.bfloat16

SUBLANES = 8
LANES = 128
MASK_BIG = 1e30
VMEM_LIMIT = 56 * 1024 * 1024
FFN_VMEM_LIMIT = 60 * 1024 * 1024

ADA_TN = 1024
ROW_TILE = 512
FFN_ROW_TILE = 1024
FFN_DOWN_COLS = 512
FF_TILE = 512
ATTN_HEADS_PER_STEP = 4
CONV_HALO = 32
CONV_CHUNK = 64

NT_DIMS = (((1,), (1,)), ((), ()))


def _params(*sem, vmem_limit=VMEM_LIMIT):
    return pltpu.CompilerParams(dimension_semantics=sem, vmem_limit_bytes=vmem_limit)


def _ada_kernel(c_ref, w_ref, b_ref, o_ref):
    c = c_ref[...]
    cs = (c * jax.nn.sigmoid(c)).astype(BF16)
    o_ref[...] = jnp.dot(cs, w_ref[...].astype(BF16), preferred_element_type=F32) + b_ref[...]


def _ada(c, w_ada, b_ada):
    B, D = c.shape
    N = w_ada.shape[1]
    return pl.pallas_call(
        _ada_kernel,
        grid=(N // ADA_TN,),
        in_specs=[
            pl.BlockSpec((B, D), lambda j: (0, 0)),
            pl.BlockSpec((D, ADA_TN), lambda j: (0, j)),
            pl.BlockSpec((1, ADA_TN), lambda j: (0, j)),
        ],
        out_specs=pl.BlockSpec((B, ADA_TN), lambda j: (0, j)),
        out_shape=jax.ShapeDtypeStruct((B, N), F32),
        compiler_params=_params("arbitrary"),
        name="ada",
    )(c, w_ada, b_ada.reshape(1, N))


def _norm_modulate(x, g, shift, scale):
    ms = jnp.mean(x * x, axis=-1, keepdims=True)
    y = x * lax.rsqrt(ms + EPS) * g
    return y * (1.0 + scale) + shift


def _inproj_kernel(x_ref, g_ref, sh_ref, sc_ref, w_ref, rope_ref, cw_ref, cb_ref, lg_ref, lb_ref,
                   qkv_ref, conv_ref, u_scr, win_scr, halo_scr, shift_scr, y_scr):
    W = ATTN_WIDTH
    tm = x_ref.shape[1]
    C = conv_ref.shape[1]
    ns = C // LANES

    @pl.when(pl.program_id(1) == 0)
    def _():
        halo_scr[...] = jnp.zeros(halo_scr.shape, F32)

    u_scr[...] = _norm_modulate(x_ref[0], g_ref[...], sh_ref[0], sc_ref[0]).astype(BF16)

    a = jnp.dot(u_scr[...], w_ref[:, 3 * W:3 * W + C], preferred_element_type=F32)
    b = jnp.dot(u_scr[...], w_ref[:, 3 * W + C:3 * W + 2 * C], preferred_element_type=F32)
    glu = a * jax.nn.sigmoid(b)
    for cs in range(ns):
        lanes = slice(cs * LANES, (cs + 1) * LANES)
        win_scr[cs, 0:CONV_HALO, :] = halo_scr[cs]
        win_scr[cs, CONV_HALO:CONV_HALO + tm, :] = glu[:, lanes]
        halo_scr[cs] = glu[tm - CONV_HALO:tm, lanes]

    for part in range(2):
        p = jnp.dot(u_scr[...], w_ref[:, part * W:(part + 1) * W], preferred_element_type=F32)
        cos = rope_ref[:, (2 * part) * HEAD_DIM:(2 * part + 1) * HEAD_DIM]
        sin = rope_ref[:, (2 * part + 1) * HEAD_DIM:(2 * part + 2) * HEAD_DIM]
        for h in range(ATTN_HEADS):
            ph = p[:, h * HEAD_DIM:(h + 1) * HEAD_DIM]
            r = ph * cos + pltpu.roll(ph, HEAD_DIM // 2, axis=1) * sin
            qkv_ref[:, part * W + h * HEAD_DIM: part * W + (h + 1) * HEAD_DIM] = r.astype(BF16)
    v = jnp.dot(u_scr[...], w_ref[:, 2 * W:3 * W], preferred_element_type=F32)
    qkv_ref[:, 2 * W:3 * W] = v.astype(BF16)

    ext = tm + CONV_HALO - SUBLANES
    lead = CONV_HALO - (CONV_KERNEL - 1)
    for cs in range(ns):
        lanes = slice(cs * LANES, (cs + 1) * LANES)
        slot = cs % 2
        for r in range(1, SUBLANES):
            shift_scr[slot, r - 1, 0:ext, :] = win_scr[cs, r:r + ext, :]
        taps_w = [jnp.broadcast_to(cw_ref[j:j + 1, lanes], (CONV_CHUNK, LANES)) for j in range(CONV_KERNEL)]
        bias = jnp.broadcast_to(cb_ref[:, lanes], (CONV_CHUNK, LANES))
        for ci in range(tm // CONV_CHUNK):
            acc = bias
            for j in range(CONV_KERNEL):
                s = lead + j
                r, t0 = s % SUBLANES, ci * CONV_CHUNK + (s // SUBLANES) * SUBLANES
                if r == 0:
                    tap = win_scr[cs, t0:t0 + CONV_CHUNK, :]
                else:
                    tap = shift_scr[slot, r - 1, t0:t0 + CONV_CHUNK, :]
                acc = acc + tap * taps_w[j]
            y_scr[ci * CONV_CHUNK:(ci + 1) * CONV_CHUNK, lanes] = acc

    y = y_scr[...]
    mu = jnp.mean(y, axis=-1, keepdims=True)
    d = y - mu
    var = jnp.mean(d * d, axis=-1, keepdims=True)
    yn = d * lax.rsqrt(var + EPS) * lg_ref[...] + lb_ref[...]
    conv_ref[...] = (yn * jax.nn.sigmoid(yn)).astype(conv_ref.dtype)


def _inproj(x, g_mix, sh_m, sc_m, w_in_bf, rope_tab, conv_w, conv_b, ln_g, ln_b):
    B, T, D = x.shape
    NW = w_in_bf.shape[1]
    cw = (NW - 3 * ATTN_WIDTH) // 2
    tm = ROW_TILE
    nt = T // tm
    vec = pl.BlockSpec((1, 1, D), lambda b, i: (b, 0, 0))
    cvec = pl.BlockSpec((1, cw), lambda b, i: (0, 0))
    rows = lambda b, i: (b * nt + i, 0)
    return pl.pallas_call(
        _inproj_kernel,
        grid=(B, nt),
        in_specs=[
            pl.BlockSpec((1, tm, D), lambda b, i: (b, i, 0)),
            pl.BlockSpec((1, D), lambda b, i: (0, 0)),
            vec, vec,
            pl.BlockSpec((D, NW), lambda b, i: (0, 0), pipeline_mode=pl.Buffered(1)),
            pl.BlockSpec((tm, 4 * HEAD_DIM), lambda b, i: (i, 0)),
            pl.BlockSpec((CONV_KERNEL, cw), lambda b, i: (0, 0)),
            cvec, cvec, cvec,
        ],
        out_specs=[
            pl.BlockSpec((tm, 3 * ATTN_WIDTH), rows),
            pl.BlockSpec((tm, cw), rows),
        ],
        out_shape=[
            jax.ShapeDtypeStruct((B * T, 3 * ATTN_WIDTH), BF16),
            jax.ShapeDtypeStruct((B * T, cw), BF16),
        ],
        scratch_shapes=[
            pltpu.VMEM((tm, D), BF16),
            pltpu.VMEM((cw // LANES, tm + CONV_HALO, LANES), F32),
            pltpu.VMEM((cw // LANES, CONV_HALO, LANES), F32),
            pltpu.VMEM((2, SUBLANES - 1, tm + CONV_HALO, LANES), F32),
            pltpu.VMEM((tm, cw), F32),
        ],
        compiler_params=_params("arbitrary", "arbitrary"),
        name="inproj",
    )(x, g_mix.reshape(1, D), sh_m, sc_m, w_in_bf, rope_tab,
      conv_w, conv_b.reshape(1, cw), ln_g.reshape(1, cw), ln_b.reshape(1, cw))


def _attn_kernel(q_ref, k_ref, v_ref, o_ref, vt_scr, st_scr, pt_scr):
    T = q_ref.shape[0]
    BLK = MOBA_BLOCK
    nb = T // BLK
    nh = q_ref.shape[1] // HEAD_DIM
    ones_rows = vt_scr.shape[1] - HEAD_DIM
    topk = min(MOBA_TOPK, nb - 1)

    row = lax.broadcasted_iota(jnp.int32, (BLK, BLK), 0)
    col = lax.broadcasted_iota(jnp.int32, (BLK, BLK), 1)
    causal = row <= col
    blk_id = lax.broadcasted_iota(jnp.int32, (nb, BLK), 0)

    def prologue(h):
        hd = slice(h * HEAD_DIM, (h + 1) * HEAD_DIM)
        vt_scr[h, 0:HEAD_DIM, :] = v_ref[:, hd].astype(F32).T.astype(BF16)
        vt_scr[h, HEAD_DIM:, :] = jnp.ones((ones_rows, T), BF16)
        kmean = jnp.mean(k_ref[:, hd].astype(F32).reshape(nb, BLK, HEAD_DIM), axis=1)
        km_hi = kmean.astype(BF16).astype(F32)
        km2 = jnp.concatenate([km_hi, kmean - km_hi], axis=0).astype(BF16)
        g2 = lax.dot_general(km2, q_ref[:, hd], NT_DIMS, preferred_element_type=F32)
        return g2[0:nb, :] + g2[nb:2 * nb, :]

    def scores(h, gate, i, slot):
        hd = slice(h * HEAD_DIM, (h + 1) * HEAD_DIM)
        qi = q_ref[i * BLK:(i + 1) * BLK, hd]
        if i <= topk:
            bias = None
        else:
            g = gate[:, i * BLK:(i + 1) * BLK]
            rank = jnp.zeros((nb, BLK), jnp.int32)
            for n2 in range(i):
                gn = g[n2:n2 + 1, :]
                beats = (gn > g) | ((gn == g) & (n2 < blk_id))
                rank = rank + beats.astype(jnp.int32)
            bias = jnp.where(rank < topk, 0.0, -MASK_BIG).astype(F32)
        m = None
        for j in range(i + 1):
            s = lax.dot_general(k_ref[j * BLK:(j + 1) * BLK, hd], qi, NT_DIMS, preferred_element_type=F32)
            if j == i:
                s = jnp.where(causal, s, -MASK_BIG)
            st_scr[slot, j * BLK:(j + 1) * BLK, :] = s
            tmax = jnp.max(s, axis=0, keepdims=True)
            if bias is not None and j < i:
                tmax = tmax + bias[j:j + 1, :]
            m = tmax if m is None else jnp.maximum(m, tmax)
        return m, bias

    def probs(i, slot, m, bias):
        for j in range(i + 1):
            c = m
            if bias is not None and j < i:
                c = m - bias[j:j + 1, :]
            p = jnp.exp2(st_scr[slot, j * BLK:(j + 1) * BLK, :] - c)
            pt_scr[slot, j * BLK:(j + 1) * BLK, :] = p.astype(BF16)

    def output(h, i, slot):
        kv = (i + 1) * BLK
        acc = jnp.dot(vt_scr[h, :, 0:kv], pt_scr[slot, 0:kv, :], preferred_element_type=F32)
        inv_l = 1.0 / acc[HEAD_DIM:HEAD_DIM + 1, :]
        o_ref[i * BLK:(i + 1) * BLK, h * HEAD_DIM:(h + 1) * HEAD_DIM] = (acc[0:HEAD_DIM, :] * inv_l).T.astype(o_ref.dtype)

    pairs = []
    lo, hi = 0, nb - 1
    while lo < hi:
        pairs.append((lo, hi))
        lo, hi = lo + 1, hi - 1
    if lo == hi:
        pairs.append((lo,))
    gates = [prologue(h) for h in range(nh)]
    for group in pairs:
        chains = [(h, i) for h in range(nh) for i in group]
        stats = [scores(h, gates[h], i, slot) for slot, (h, i) in enumerate(chains)]
        for slot, (h, i) in enumerate(chains):
            probs(i, slot, *stats[slot])
        for slot, (h, i) in enumerate(chains):
            output(h, i, slot)


def _attn(qkv, B, T):
    H = ATTN_HEADS
    nh = ATTN_HEADS_PER_STEP
    hw = nh * HEAD_DIM
    ones_rows = 16
    return pl.pallas_call(
        _attn_kernel,
        grid=(B, H // nh),
        in_specs=[
            pl.BlockSpec((T, hw), lambda b, h: (b, h)),
            pl.BlockSpec((T, hw), lambda b, h: (b, H // nh + h)),
            pl.BlockSpec((T, hw), lambda b, h: (b, 2 * (H // nh) + h)),
        ],
        out_specs=pl.BlockSpec((T, hw), lambda b, h: (b, h)),
        out_shape=jax.ShapeDtypeStruct((B * T, ATTN_WIDTH), BF16),
        scratch_shapes=[
            pltpu.VMEM((nh, HEAD_DIM + ones_rows, T), BF16),
            pltpu.VMEM((2 * nh, T, MOBA_BLOCK), F32),
            pltpu.VMEM((2 * nh, T, MOBA_BLOCK), BF16),
        ],
        compiler_params=_params("arbitrary", "arbitrary"),
        name="attn",
    )(qkv, qkv, qkv)


def _outproj_kernel(attn_ref, conv_ref, w_ref, x_ref, gt_ref, g_ref, sh_ref, sc_ref, h_ref, u_ref):
    aw = attn_ref.shape[1]
    mixed = (jnp.dot(attn_ref[...], w_ref[0:aw, :], preferred_element_type=F32)
             + jnp.dot(conv_ref[...], w_ref[aw:, :], preferred_element_type=F32))
    h = x_ref[0] + gt_ref[0] * mixed
    h_ref[...] = h
    u_ref[...] = _norm_modulate(h, g_ref[...], sh_ref[0], sc_ref[0]).astype(BF16)


def _outproj(attn, conv, w_out_bf, x, gt_m, g_ffn, sh_f, sc_f):
    B, T, D = x.shape
    aw, cw = attn.shape[1], conv.shape[1]
    tm = ROW_TILE
    nt = T // tm
    vec = pl.BlockSpec((1, 1, D), lambda b, i: (b, 0, 0))
    rows = lambda b, i: (b * nt + i, 0)
    return pl.pallas_call(
        _outproj_kernel,
        grid=(B, nt),
        in_specs=[
            pl.BlockSpec((tm, aw), rows),
            pl.BlockSpec((tm, cw), rows),
            pl.BlockSpec((D, D), lambda b, i: (0, 0), pipeline_mode=pl.Buffered(1)),
            pl.BlockSpec((1, tm, D), lambda b, i: (b, i, 0)),
            vec,
            pl.BlockSpec((1, D), lambda b, i: (0, 0)),
            vec, vec,
        ],
        out_specs=[pl.BlockSpec((tm, D), rows), pl.BlockSpec((tm, D), rows)],
        out_shape=[jax.ShapeDtypeStruct((B * T, D), F32), jax.ShapeDtypeStruct((B * T, D), BF16)],
        compiler_params=_params("arbitrary", "arbitrary"),
        name="outproj",
    )(attn, conv, w_out_bf, x, gt_m, g_ffn.reshape(1, D), sh_f, sc_f)


def _ffn_kernel(u_ref, wg_ref, wu_ref, wd_ref, h_hbm, gt_ref, g_ref, o_ref, h_scr, h_sem):
    b, i, f = pl.program_id(0), pl.program_id(1), pl.program_id(2)
    tm = u_ref.shape[0]
    nf = pl.num_programs(2)
    row0 = pl.multiple_of((b * pl.num_programs(1) + i) * tm, tm)
    h_copy = pltpu.make_async_copy(h_hbm.at[pl.ds(row0, tm), :], h_scr, h_sem.at[0])

    @pl.when(f == 0)
    def _():
        h_copy.start()
        o_ref[0] = jnp.zeros(o_ref.shape[1:], F32)

    u = u_ref[...]
    gte = jnp.dot(u, wg_ref[...], preferred_element_type=F32)
    up = jnp.dot(u, wu_ref[...], preferred_element_type=F32)
    a = (gte * jax.nn.sigmoid(gte) * up).astype(BF16)
    D = o_ref.shape[2]
    for n0 in range(0, D, FFN_DOWN_COLS):
        cols = slice(n0, n0 + FFN_DOWN_COLS)
        o_ref[0, :, cols] += jnp.dot(a, wd_ref[:, cols], preferred_element_type=F32)

    @pl.when(f == nf - 1)
    def _():
        h_copy.wait()
        h = h_scr[...] + gt_ref[0] * o_ref[0]
        ms = jnp.mean(h * h, axis=-1, keepdims=True)
        o_ref[0] = h * lax.rsqrt(ms + EPS) * g_ref[...]


def _ffn(u2, wg_bf, wu_bf, wd_bf, h1, gt_f, g_final, B, T):
    D = u2.shape[1]
    FF = wg_bf.shape[1]
    tm, tf = FFN_ROW_TILE, FF_TILE
    nt = T // tm
    rows = lambda b, i, f: (b * nt + i, 0)
    return pl.pallas_call(
        _ffn_kernel,
        grid=(B, nt, FF // tf),
        in_specs=[
            pl.BlockSpec((tm, D), rows),
            pl.BlockSpec((D, tf), lambda b, i, f: (0, f)),
            pl.BlockSpec((D, tf), lambda b, i, f: (0, f)),
            pl.BlockSpec((tf, D), lambda b, i, f: (f, 0)),
            pl.BlockSpec(memory_space=pl.ANY),
            pl.BlockSpec((1, 1, D), lambda b, i, f: (b, 0, 0)),
            pl.BlockSpec((1, D), lambda b, i, f: (0, 0)),
        ],
        out_specs=pl.BlockSpec((1, tm, D), lambda b, i, f: (b, i, 0)),
        out_shape=jax.ShapeDtypeStruct((B, T, D), F32),
        scratch_shapes=[pltpu.VMEM((tm, D), F32), pltpu.SemaphoreType.DMA((1,))],
        compiler_params=_params("arbitrary", "arbitrary", "arbitrary", vmem_limit=FFN_VMEM_LIMIT),
        name="ffn",
    )(u2, wg_bf, wu_bf, wd_bf, h1, gt_f, g_final.reshape(1, D))


def _rope_tables(T):
    half = HEAD_DIM // 2
    inv = ROPE_THETA ** (-jnp.arange(half, dtype=F32) / half)
    ang = jnp.arange(T, dtype=F32)[:, None] * inv[None, :]
    cos, sin = jnp.cos(ang), jnp.sin(ang)
    cos_f = jnp.concatenate([cos, cos], axis=-1)
    sin_f = jnp.concatenate([-sin, sin], axis=-1)
    qs = (HEAD_DIM ** -0.5) * math.log2(math.e)
    return jnp.concatenate([cos_f * qs, sin_f * qs, cos_f, sin_f], axis=-1)


def kernel(x, c, w_ada, b_ada, g_mix, w_in, conv_w, conv_b, ln_g, ln_b, w_out, g_ffn, w_gate, w_up, w_down, g_final):
    B, T, D = x.shape
    depth = w_ada.shape[0]
    assert depth == 1 and T % MOBA_BLOCK == 0 and T % ROW_TILE == 0 and T % FFN_ROW_TILE == 0 and D % LANES == 0
    rope_tab = _rope_tables(T)
    h = x
    for l in range(depth):
        mod = _ada(c, w_ada[l], b_ada[l])
        sh_m, sc_m, gt_m, sh_f, sc_f, gt_f = [m.reshape(B, 1, D) for m in jnp.split(mod, N_MOD, axis=-1)]
        qkv, conv = _inproj(h, g_mix[l], sh_m, sc_m, w_in[l].astype(BF16), rope_tab,
                            conv_w[l], conv_b[l], ln_g[l], ln_b[l])
        attn = _attn(qkv, B, T)
        h1, u2 = _outproj(attn, conv, w_out[l].astype(BF16), h, gt_m, g_ffn[l], sh_f, sc_f)
        out = _ffn(u2, w_gate[l].astype(BF16), w_up[l].astype(BF16), w_down[l].astype(BF16), h1, gt_f, g_final, B, T)
    return out
```

```python
import functools
import math

import jax
import jax.numpy as jnp
from jax import lax
from jax.experimental import pallas as pl
from jax.experimental.pallas import tpu as pltpu

ATTN_HEADS = 8
HEAD_DIM = 128
ATTN_WIDTH = ATTN_HEADS * HEAD_DIM
CONV_KERNEL = 31
MOBA_BLOCK = 256
MOBA_TOPK = 3
ROPE_THETA = 10000.0
N_MOD = 6
EPS = 1e-6

F32 = jnp.float32
BF16 = jnp.bfloat16

SUBLANES = 8
LANES = 128
MASK_BIG = 1e30
VMEM_LIMIT = 56 * 1024 * 1024
BIG_VMEM_LIMIT = 60 * 1024 * 1024

ADA_TN = 1024
ROW_TILE = 512
FFN_ROW_TILE = 1024
FFN_DOWN_COLS = 512
FF_TILE = 512
ATTN_HEADS_PER_STEP = 4
CONV_HALO = 32
CONV_CHUNK = 64

NT_DIMS = (((1,), (1,)), ((), ()))


def _params(*sem, vmem_limit=VMEM_LIMIT):
    return pltpu.CompilerParams(dimension_semantics=sem, vmem_limit_bytes=vmem_limit)


def _ada_kernel(c_ref, w_ref, b_ref, o_ref):
    c = c_ref[...]
    cs = (c * jax.nn.sigmoid(c)).astype(BF16)
    o_ref[...] = jnp.dot(cs, w_ref[...].astype(BF16), preferred_element_type=F32) + b_ref[...]


def _ada(c, w_ada, b_ada):
    B, D = c.shape
    N = w_ada.shape[1]
    return pl.pallas_call(
        _ada_kernel,
        grid=(N // ADA_TN,),
        in_specs=[
            pl.BlockSpec((B, D), lambda j: (0, 0)),
            pl.BlockSpec((D, ADA_TN), lambda j: (0, j)),
            pl.BlockSpec((1, ADA_TN), lambda j: (0, j)),
        ],
        out_specs=pl.BlockSpec((B, ADA_TN), lambda j: (0, j)),
        out_shape=jax.ShapeDtypeStruct((B, N), F32),
        compiler_params=_params("arbitrary"),
        name="ada",
    )(c, w_ada, b_ada.reshape(1, N))


def _norm_modulate(x, g, shift, scale):
    ms = jnp.mean(x * x, axis=-1, keepdims=True)
    y = x * lax.rsqrt(ms + EPS) * g
    return y * (1.0 + scale) + shift


def _inproj_kernel(x_ref, g_ref, sh_ref, sc_ref, w_ref, rope_ref, cw_ref, cb_ref, lg_ref, lb_ref,
                   qkv_ref, conv_ref, u_scr, win_scr, halo_scr, shift_scr, y_scr):
    W = ATTN_WIDTH
    tm = x_ref.shape[1]
    C = conv_ref.shape[1]
    ns = C // LANES

    @pl.when(pl.program_id(1) == 0)
    def _():
        halo_scr[...] = jnp.zeros(halo_scr.shape, F32)

    u_scr[...] = _norm_modulate(x_ref[0], g_ref[...], sh_ref[0], sc_ref[0]).astype(BF16)

    a = jnp.dot(u_scr[...], w_ref[:, 3 * W:3 * W + C], preferred_element_type=F32)
    b = jnp.dot(u_scr[...], w_ref[:, 3 * W + C:3 * W + 2 * C], preferred_element_type=F32)
    glu = a * jax.nn.sigmoid(b)
    for cs in range(ns):
        lanes = slice(cs * LANES, (cs + 1) * LANES)
        win_scr[cs, 0:CONV_HALO, :] = halo_scr[cs]
        win_scr[cs, CONV_HALO:CONV_HALO + tm, :] = glu[:, lanes]
        halo_scr[cs] = glu[tm - CONV_HALO:tm, lanes]

    for part in range(2):
        p = jnp.dot(u_scr[...], w_ref[:, part * W:(part + 1) * W], preferred_element_type=F32)
        cos = rope_ref[:, (2 * part) * HEAD_DIM:(2 * part + 1) * HEAD_DIM]
        sin = rope_ref[:, (2 * part + 1) * HEAD_DIM:(2 * part + 2) * HEAD_DIM]
        for h in range(ATTN_HEADS):
            ph = p[:, h * HEAD_DIM:(h + 1) * HEAD_DIM]
            r = ph * cos + pltpu.roll(ph, HEAD_DIM // 2, axis=1) * sin
            qkv_ref[:, part * W + h * HEAD_DIM: part * W + (h + 1) * HEAD_DIM] = r.astype(BF16)
    v = jnp.dot(u_scr[...], w_ref[:, 2 * W:3 * W], preferred_element_type=F32)
    qkv_ref[:, 2 * W:3 * W] = v.astype(BF16)

    ext = tm + CONV_HALO - SUBLANES
    lead = CONV_HALO - (CONV_KERNEL - 1)
    for cs in range(ns):
        lanes = slice(cs * LANES, (cs + 1) * LANES)
        slot = cs % 2
        for r in range(1, SUBLANES):
            shift_scr[slot, r - 1, 0:ext, :] = win_scr[cs, r:r + ext, :]
        taps_w = [jnp.broadcast_to(cw_ref[j:j + 1, lanes], (CONV_CHUNK, LANES)) for j in range(CONV_KERNEL)]
        bias = jnp.broadcast_to(cb_ref[:, lanes], (CONV_CHUNK, LANES))
        for ci in range(tm // CONV_CHUNK):
            acc = bias
            for j in range(CONV_KERNEL):
                s = lead + j
                r, t0 = s % SUBLANES, ci * CONV_CHUNK + (s // SUBLANES) * SUBLANES
                if r == 0:
                    tap = win_scr[cs, t0:t0 + CONV_CHUNK, :]
                else:
                    tap = shift_scr[slot, r - 1, t0:t0 + CONV_CHUNK, :]
                acc = acc + tap * taps_w[j]
            y_scr[ci * CONV_CHUNK:(ci + 1) * CONV_CHUNK, lanes] = acc

    y = y_scr[...]
    mu = jnp.mean(y, axis=-1, keepdims=True)
    d = y - mu
    var = jnp.mean(d * d, axis=-1, keepdims=True)
    yn = d * lax.rsqrt(var + EPS) * lg_ref[...] + lb_ref[...]
    conv_ref[...] = (yn * jax.nn.sigmoid(yn)).astype(conv_ref.dtype)


def _inproj(x, g_mix, sh_m, sc_m, w_in_bf, rope_tab, conv_w, conv_b, ln_g, ln_b):
    B, T, D = x.shape
    NW = w_in_bf.shape[1]
    cw = (NW - 3 * ATTN_WIDTH) // 2
    tm = ROW_TILE
    nt = T // tm
    vec = pl.BlockSpec((1, 1, D), lambda b, i: (b, 0, 0))
    cvec = pl.BlockSpec((1, cw), lambda b, i: (0, 0))
    rows = lambda b, i: (b * nt + i, 0)
    return pl.pallas_call(
        _inproj_kernel,
        grid=(B, nt),
        in_specs=[
            pl.BlockSpec((1, tm, D), lambda b, i: (b, i, 0)),
            pl.BlockSpec((1, D), lambda b, i: (0, 0)),
            vec, vec,
            pl.BlockSpec((D, NW), lambda b, i: (0, 0), pipeline_mode=pl.Buffered(1)),
            pl.BlockSpec((tm, 4 * HEAD_DIM), lambda b, i: (i, 0)),
            pl.BlockSpec((CONV_KERNEL, cw), lambda b, i: (0, 0)),
            cvec, cvec, cvec,
        ],
        out_specs=[
            pl.BlockSpec((tm, 3 * ATTN_WIDTH), rows),
            pl.BlockSpec((tm, cw), rows),
        ],
        out_shape=[
            jax.ShapeDtypeStruct((B * T, 3 * ATTN_WIDTH), BF16),
            jax.ShapeDtypeStruct((B * T, cw), BF16),
        ],
        scratch_shapes=[
            pltpu.VMEM((tm, D), BF16),
            pltpu.VMEM((cw // LANES, tm + CONV_HALO, LANES), F32),
            pltpu.VMEM((cw // LANES, CONV_HALO, LANES), F32),
            pltpu.VMEM((2, SUBLANES - 1, tm + CONV_HALO, LANES), F32),
            pltpu.VMEM((tm, cw), F32),
        ],
        compiler_params=_params("arbitrary", "arbitrary"),
        name="inproj",
    )(x, g_mix.reshape(1, D), sh_m, sc_m, w_in_bf, rope_tab,
      conv_w, conv_b.reshape(1, cw), ln_g.reshape(1, cw), ln_b.reshape(1, cw))


def _attn_kernel(q_ref, k_ref, v_ref, *rest):
    n_w = (len(rest) - 4) // 2
    w_refs, o_ref, wbf_refs = rest[:n_w], rest[n_w], rest[n_w + 1:2 * n_w + 1]
    vt_scr, st_scr, pt_scr = rest[2 * n_w + 1:]
    for w_ref, wbf_ref in zip(w_refs, wbf_refs):
        wbf_ref[...] = w_ref[...].astype(BF16)

    T = q_ref.shape[0]
    BLK = MOBA_BLOCK
    nb = T // BLK
    nh = q_ref.shape[1] // HEAD_DIM
    ones_rows = vt_scr.shape[1] - HEAD_DIM
    topk = min(MOBA_TOPK, nb - 1)

    row = lax.broadcasted_iota(jnp.int32, (BLK, BLK), 0)
    col = lax.broadcasted_iota(jnp.int32, (BLK, BLK), 1)
    causal = row <= col
    blk_id = lax.broadcasted_iota(jnp.int32, (nb, BLK), 0)

    def prologue(h):
        hd = slice(h * HEAD_DIM, (h + 1) * HEAD_DIM)
        vt_scr[h, 0:HEAD_DIM, :] = v_ref[:, hd].astype(F32).T.astype(BF16)
        vt_scr[h, HEAD_DIM:, :] = jnp.ones((ones_rows, T), BF16)
        kmean = jnp.mean(k_ref[:, hd].astype(F32).reshape(nb, BLK, HEAD_DIM), axis=1)
        km_hi = kmean.astype(BF16).astype(F32)
        km2 = jnp.concatenate([km_hi, kmean - km_hi], axis=0).astype(BF16)
        g2 = lax.dot_general(km2, q_ref[:, hd], NT_DIMS, preferred_element_type=F32)
        return g2[0:nb, :] + g2[nb:2 * nb, :]

    def scores(h, gate, i, slot):
        hd = slice(h * HEAD_DIM, (h + 1) * HEAD_DIM)
        qi = q_ref[i * BLK:(i + 1) * BLK, hd]
        if i <= topk:
            bias = None
        else:
            g = gate[:, i * BLK:(i + 1) * BLK]
            rank = jnp.zeros((nb, BLK), jnp.int32)
            for n2 in range(i):
                gn = g[n2:n2 + 1, :]
                beats = (gn > g) | ((gn == g) & (n2 < blk_id))
                rank = rank + beats.astype(jnp.int32)
            bias = jnp.where(rank < topk, 0.0, -MASK_BIG).astype(F32)
        m = None
        for j in range(i + 1):
            s = lax.dot_general(k_ref[j * BLK:(j + 1) * BLK, hd], qi, NT_DIMS, preferred_element_type=F32)
            if j == i:
                s = jnp.where(causal, s, -MASK_BIG)
            st_scr[slot, j * BLK:(j + 1) * BLK, :] = s
            tmax = jnp.max(s, axis=0, keepdims=True)
            if bias is not None and j < i:
                tmax = tmax + bias[j:j + 1, :]
            m = tmax if m is None else jnp.maximum(m, tmax)
        return m, bias

    def probs(i, slot, m, bias):
        for j in range(i + 1):
            c = m
            if bias is not None and j < i:
                c = m - bias[j:j + 1, :]
            p = jnp.exp2(st_scr[slot, j * BLK:(j + 1) * BLK, :] - c)
            pt_scr[slot, j * BLK:(j + 1) * BLK, :] = p.astype(BF16)

    def output(h, i, slot):
        kv = (i + 1) * BLK
        acc = jnp.dot(vt_scr[h, :, 0:kv], pt_scr[slot, 0:kv, :], preferred_element_type=F32)
        inv_l = 1.0 / acc[HEAD_DIM:HEAD_DIM + 1, :]
        o_ref[i * BLK:(i + 1) * BLK, h * HEAD_DIM:(h + 1) * HEAD_DIM] = (acc[0:HEAD_DIM, :] * inv_l).T.astype(o_ref.dtype)

    pairs = []
    lo, hi = 0, nb - 1
    while lo < hi:
        pairs.append((lo, hi))
        lo, hi = lo + 1, hi - 1
    if lo == hi:
        pairs.append((lo,))
    gates = [prologue(h) for h in range(nh)]
    for group in pairs:
        chains = [(h, i) for h in range(nh) for i in group]
        stats = [scores(h, gates[h], i, slot) for slot, (h, i) in enumerate(chains)]
        for slot, (h, i) in enumerate(chains):
            probs(i, slot, *stats[slot])
        for slot, (h, i) in enumerate(chains):
            output(h, i, slot)


def _attn(qkv, B, T, weights):
    H = ATTN_HEADS
    nh = ATTN_HEADS_PER_STEP
    hw = nh * HEAD_DIM
    ones_rows = 16
    steps = B * (H // nh)
    for w in weights:
        assert w.shape[0] % (steps * 2 * SUBLANES) == 0, w.shape
    flat = lambda b, h: (b * (H // nh) + h, 0)
    w_specs = [pl.BlockSpec((w.shape[0] // steps, w.shape[1]), flat) for w in weights]
    outs = pl.pallas_call(
        _attn_kernel,
        grid=(B, H // nh),
        in_specs=[
            pl.BlockSpec((T, hw), lambda b, h: (b, h)),
            pl.BlockSpec((T, hw), lambda b, h: (b, H // nh + h)),
            pl.BlockSpec((T, hw), lambda b, h: (b, 2 * (H // nh) + h)),
        ] + w_specs,
        out_specs=[pl.BlockSpec((T, hw), lambda b, h: (b, h))] + w_specs,
        out_shape=[jax.ShapeDtypeStruct((B * T, ATTN_WIDTH), BF16)]
                  + [jax.ShapeDtypeStruct(w.shape, BF16) for w in weights],
        scratch_shapes=[
            pltpu.VMEM((nh, HEAD_DIM + ones_rows, T), BF16),
            pltpu.VMEM((2 * nh, T, MOBA_BLOCK), F32),
            pltpu.VMEM((2 * nh, T, MOBA_BLOCK), BF16),
        ],
        compiler_params=_params("arbitrary", "arbitrary", vmem_limit=BIG_VMEM_LIMIT),
        name="attn",
    )(qkv, qkv, qkv, *weights)
    return outs[0], outs[1:]


def _outproj_kernel(attn_ref, conv_ref, w_ref, x_ref, gt_ref, g_ref, sh_ref, sc_ref, h_ref, u_ref):
    aw = attn_ref.shape[1]
    mixed = (jnp.dot(attn_ref[...], w_ref[0:aw, :], preferred_element_type=F32)
             + jnp.dot(conv_ref[...], w_ref[aw:, :], preferred_element_type=F32))
    h = x_ref[0] + gt_ref[0] * mixed
    h_ref[...] = h
    u_ref[...] = _norm_modulate(h, g_ref[...], sh_ref[0], sc_ref[0]).astype(BF16)


def _outproj(attn, conv, w_out_bf, x, gt_m, g_ffn, sh_f, sc_f):
    B, T, D = x.shape
    aw, cw = attn.shape[1], conv.shape[1]
    tm = ROW_TILE
    nt = T // tm
    vec = pl.BlockSpec((1, 1, D), lambda b, i: (b, 0, 0))
    rows = lambda b, i: (b * nt + i, 0)
    return pl.pallas_call(
        _outproj_kernel,
        grid=(B, nt),
        in_specs=[
            pl.BlockSpec((tm, aw), rows),
            pl.BlockSpec((tm, cw), rows),
            pl.BlockSpec((D, D), lambda b, i: (0, 0), pipeline_mode=pl.Buffered(1)),
            pl.BlockSpec((1, tm, D), lambda b, i: (b, i, 0)),
            vec,
            pl.BlockSpec((1, D), lambda b, i: (0, 0)),
            vec, vec,
        ],
        out_specs=[pl.BlockSpec((tm, D), rows), pl.BlockSpec((tm, D), rows)],
        out_shape=[jax.ShapeDtypeStruct((B * T, D), F32), jax.ShapeDtypeStruct((B * T, D), BF16)],
        compiler_params=_params("arbitrary", "arbitrary"),
        name="outproj",
    )(attn, conv, w_out_bf, x, gt_m, g_ffn.reshape(1, D), sh_f, sc_f)


def _ffn_kernel(u_ref, wg_ref, wu_ref, wd_ref, h_hbm, gt_ref, g_ref, o_ref, h_scr, h_sem):
    b, i, f = pl.program_id(0), pl.program_id(1), pl.program_id(2)
    tm = u_ref.shape[0]
    nf = pl.num_programs(2)
    row0 = pl.multiple_of((b * pl.num_programs(1) + i) * tm, tm)
    h_copy = pltpu.make_async_copy(h_hbm.at[pl.ds(row0, tm), :], h_scr, h_sem.at[0])

    @pl.when(f == 0)
    def _():
        h_copy.start()
        o_ref[0] = jnp.zeros(o_ref.shape[1:], F32)

    u = u_ref[...]
    gte = jnp.dot(u, wg_ref[...], preferred_element_type=F32)
    up = jnp.dot(u, wu_ref[...], preferred_element_type=F32)
    a = (gte * jax.nn.sigmoid(gte) * up).astype(BF16)
    D = o_ref.shape[2]
    for n0 in range(0, D, FFN_DOWN_COLS):
        cols = slice(n0, n0 + FFN_DOWN_COLS)
        o_ref[0, :, cols] += jnp.dot(a, wd_ref[:, cols], preferred_element_type=F32)

    @pl.when(f == nf - 1)
    def _():
        h_copy.wait()
        h = h_scr[...] + gt_ref[0] * o_ref[0]
        ms = jnp.mean(h * h, axis=-1, keepdims=True)
        o_ref[0] = h * lax.rsqrt(ms + EPS) * g_ref[...]


def _ffn(u2, wg_bf, wu_bf, wd_bf, h1, gt_f, g_final, B, T):
    D = u2.shape[1]
    FF = wg_bf.shape[1]
    tm, tf = FFN_ROW_TILE, FF_TILE
    nt = T // tm
    rows = lambda b, i, f: (b * nt + i, 0)
    return pl.pallas_call(
        _ffn_kernel,
        grid=(B, nt, FF // tf),
        in_specs=[
            pl.BlockSpec((tm, D), rows),
            pl.BlockSpec((D, tf), lambda b, i, f: (0, f)),
            pl.BlockSpec((D, tf), lambda b, i, f: (0, f)),
            pl.BlockSpec((tf, D), lambda b, i, f: (f, 0)),
            pl.BlockSpec(memory_space=pl.ANY),
            pl.BlockSpec((1, 1, D), lambda b, i, f: (b, 0, 0)),
            pl.BlockSpec((1, D), lambda b, i, f: (0, 0)),
        ],
        out_specs=pl.BlockSpec((1, tm, D), lambda b, i, f: (b, i, 0)),
        out_shape=jax.ShapeDtypeStruct((B, T, D), F32),
        scratch_shapes=[pltpu.VMEM((tm, D), F32), pltpu.SemaphoreType.DMA((1,))],
        compiler_params=_params("arbitrary", "arbitrary", "arbitrary", vmem_limit=BIG_VMEM_LIMIT),
        name="ffn",
    )(u2, wg_bf, wu_bf, wd_bf, h1, gt_f, g_final.reshape(1, D))


def _rope_tables(T):
    half = HEAD_DIM // 2
    inv = ROPE_THETA ** (-jnp.arange(half, dtype=F32) / half)
    ang = jnp.arange(T, dtype=F32)[:, None] * inv[None, :]
    cos, sin = jnp.cos(ang), jnp.sin(ang)
    cos_f = jnp.concatenate([cos, cos], axis=-1)
    sin_f = jnp.concatenate([-sin, sin], axis=-1)
    qs = (HEAD_DIM ** -0.5) * math.log2(math.e)
    return jnp.concatenate([cos_f * qs, sin_f * qs, cos_f, sin_f], axis=-1)


def kernel(x, c, w_ada, b_ada, g_mix, w_in, conv_w, conv_b, ln_g, ln_b, w_out, g_ffn, w_gate, w_up, w_down, g_final):
    B, T, D = x.shape
    depth = w_ada.shape[0]
    assert depth == 1 and T % MOBA_BLOCK == 0 and T % ROW_TILE == 0 and T % FFN_ROW_TILE == 0 and D % LANES == 0
    rope_tab = _rope_tables(T)
    h = x
    for l in range(depth):
        mod = _ada(c, w_ada[l], b_ada[l])
        sh_m, sc_m, gt_m, sh_f, sc_f, gt_f = [m.reshape(B, 1, D) for m in jnp.split(mod, N_MOD, axis=-1)]
        qkv, conv = _inproj(h, g_mix[l], sh_m, sc_m, w_in[l].astype(BF16), rope_tab,
                            conv_w[l], conv_b[l], ln_g[l], ln_b[l])
        attn, (w_out_bf, w_gate_bf, w_up_bf, w_down_bf) = _attn(
            qkv, B, T, (w_out[l], w_gate[l], w_up[l], w_down[l]))
        h1, u2 = _outproj(attn, conv, w_out_bf, h, gt_m, g_ffn[l], sh_f, sc_f)
        out = _ffn(u2, w_gate_bf, w_up_bf, w_down_bf, h1, gt_f, g_final, B, T)
    return out
```

```python
import math

import jax
import jax.numpy as jnp
from jax import lax
from jax.experimental import pallas as pl
from jax.experimental.pallas import tpu as pltpu

ATTN_HEADS = 8
HEAD_DIM = 128
ATTN_WIDTH = ATTN_HEADS * HEAD_DIM
CONV_KERNEL = 31
MOBA_BLOCK = 256
MOBA_TOPK = 3
ROPE_THETA = 10000.0
N_MOD = 6
EPS = 1e-6

F32 = jnp.float32
BF16 = jnp.bfloat16

SUBLANES = 8
LANES = 128
MASK_BIG = 1e30
VMEM_LIMIT = 56 * 1024 * 1024
BIG_VMEM_LIMIT = 60 * 1024 * 1024

ADA_TN = 768
ROW_TILE = 512
FFN_ROW_TILE = 1024
FFN_DOWN_COLS = 512
FF_TILE = 512
FFN_EPI_ROWS = SUBLANES
ATTN_HEADS_PER_STEP = 4
CONV_HALO = 32
CONV_CHUNK = 64

NT_DIMS = (((1,), (1,)), ((), ()))


def _params(*sem, vmem_limit=VMEM_LIMIT):
    return pltpu.CompilerParams(dimension_semantics=sem, vmem_limit_bytes=vmem_limit)


def _ada_kernel(c_ref, w_ref, b_ref, wi_ref, o_ref, wi_bf_ref):
    c = c_ref[...]
    cs = (c * jax.nn.sigmoid(c)).astype(BF16)
    o_ref[...] = jnp.dot(cs, w_ref[...].astype(BF16), preferred_element_type=F32) + b_ref[...]
    wi_bf_ref[...] = wi_ref[...].astype(BF16)


def _ada(c, w_ada, b_ada, w_in):
    B, D = c.shape
    N = w_ada.shape[1]
    steps = N // ADA_TN
    assert N % ADA_TN == 0 and w_in.shape[0] % (steps * 2 * SUBLANES) == 0
    wi_spec = pl.BlockSpec((w_in.shape[0] // steps, w_in.shape[1]), lambda j: (j, 0))
    return pl.pallas_call(
        _ada_kernel,
        grid=(steps,),
        in_specs=[
            pl.BlockSpec((B, D), lambda j: (0, 0)),
            pl.BlockSpec((D, ADA_TN), lambda j: (0, j)),
            pl.BlockSpec((1, ADA_TN), lambda j: (0, j)),
            wi_spec,
        ],
        out_specs=[pl.BlockSpec((B, ADA_TN), lambda j: (0, j)), wi_spec],
        out_shape=[jax.ShapeDtypeStruct((B, N), F32), jax.ShapeDtypeStruct(w_in.shape, BF16)],
        compiler_params=_params("arbitrary"),
        name="ada",
    )(c, w_ada, b_ada.reshape(1, N), w_in)


def _norm_modulate(x, g, shift, scale):
    ms = jnp.mean(x * x, axis=-1, keepdims=True)
    y = x * lax.rsqrt(ms + EPS) * g
    return y * (1.0 + scale) + shift


def _inproj_kernel(x_ref, g_ref, sh_ref, sc_ref, w_ref, rope_ref, cw_ref, cb_ref, lg_ref, lb_ref,
                   qkv_ref, conv_ref, u_scr, win_scr, halo_scr, shift_scr, y_scr):
    W = ATTN_WIDTH
    tm = x_ref.shape[1]
    C = conv_ref.shape[1]
    ns = C // LANES

    @pl.when(pl.program_id(1) == 0)
    def _():
        halo_scr[...] = jnp.zeros(halo_scr.shape, F32)

    u_scr[...] = _norm_modulate(x_ref[0], g_ref[...], sh_ref[0], sc_ref[0]).astype(BF16)

    a = jnp.dot(u_scr[...], w_ref[:, 3 * W:3 * W + C], preferred_element_type=F32)
    b = jnp.dot(u_scr[...], w_ref[:, 3 * W + C:3 * W + 2 * C], preferred_element_type=F32)
    glu = a * jax.nn.sigmoid(b)
    for cs in range(ns):
        lanes = slice(cs * LANES, (cs + 1) * LANES)
        win_scr[cs, 0:CONV_HALO, :] = halo_scr[cs]
        win_scr[cs, CONV_HALO:CONV_HALO + tm, :] = glu[:, lanes]
        halo_scr[cs] = glu[tm - CONV_HALO:tm, lanes]

    for part in range(2):
        p = jnp.dot(u_scr[...], w_ref[:, part * W:(part + 1) * W], preferred_element_type=F32)
        cos = rope_ref[:, (2 * part) * HEAD_DIM:(2 * part + 1) * HEAD_DIM]
        sin = rope_ref[:, (2 * part + 1) * HEAD_DIM:(2 * part + 2) * HEAD_DIM]
        for h in range(ATTN_HEADS):
            ph = p[:, h * HEAD_DIM:(h + 1) * HEAD_DIM]
            r = ph * cos + pltpu.roll(ph, HEAD_DIM // 2, axis=1) * sin
            qkv_ref[:, part * W + h * HEAD_DIM: part * W + (h + 1) * HEAD_DIM] = r.astype(BF16)
    v = jnp.dot(u_scr[...], w_ref[:, 2 * W:3 * W], preferred_element_type=F32)
    qkv_ref[:, 2 * W:3 * W] = v.astype(BF16)

    ext = tm + CONV_HALO - SUBLANES
    lead = CONV_HALO - (CONV_KERNEL - 1)
    for cs in range(ns):
        lanes = slice(cs * LANES, (cs + 1) * LANES)
        slot = cs % 2
        for r in range(1, SUBLANES):
            shift_scr[slot, r - 1, 0:ext, :] = win_scr[cs, r:r + ext, :]
        taps_w = [jnp.broadcast_to(cw_ref[j:j + 1, lanes], (CONV_CHUNK, LANES)) for j in range(CONV_KERNEL)]
        bias = jnp.broadcast_to(cb_ref[:, lanes], (CONV_CHUNK, LANES))
        for ci in range(tm // CONV_CHUNK):
            acc = bias
            for j in range(CONV_KERNEL):
                s = lead + j
                r, t0 = s % SUBLANES, ci * CONV_CHUNK + (s // SUBLANES) * SUBLANES
                if r == 0:
                    tap = win_scr[cs, t0:t0 + CONV_CHUNK, :]
                else:
                    tap = shift_scr[slot, r - 1, t0:t0 + CONV_CHUNK, :]
                acc = acc + tap * taps_w[j]
            y_scr[ci * CONV_CHUNK:(ci + 1) * CONV_CHUNK, lanes] = acc

    y = y_scr[...]
    mu = jnp.mean(y, axis=-1, keepdims=True)
    d = y - mu
    var = jnp.mean(d * d, axis=-1, keepdims=True)
    yn = d * lax.rsqrt(var + EPS) * lg_ref[...] + lb_ref[...]
    conv_ref[...] = (yn * jax.nn.sigmoid(yn)).astype(conv_ref.dtype)


def _inproj(x, g_mix, sh_m, sc_m, w_in_bf, rope_tab, conv_w, conv_b, ln_g, ln_b):
    B, T, D = x.shape
    NW = w_in_bf.shape[1]
    cw = (NW - 3 * ATTN_WIDTH) // 2
    tm = ROW_TILE
    nt = T // tm
    vec = pl.BlockSpec((1, 1, D), lambda b, i: (b, 0, 0))
    cvec = pl.BlockSpec((1, cw), lambda b, i: (0, 0))
    rows = lambda b, i: (b * nt + i, 0)
    return pl.pallas_call(
        _inproj_kernel,
        grid=(B, nt),
        in_specs=[
            pl.BlockSpec((1, tm, D), lambda b, i: (b, i, 0)),
            pl.BlockSpec((1, D), lambda b, i: (0, 0)),
            vec, vec,
            pl.BlockSpec((D, NW), lambda b, i: (0, 0), pipeline_mode=pl.Buffered(1)),
            pl.BlockSpec((tm, 4 * HEAD_DIM), lambda b, i: (i, 0)),
            pl.BlockSpec((CONV_KERNEL, cw), lambda b, i: (0, 0)),
            cvec, cvec, cvec,
        ],
        out_specs=[
            pl.BlockSpec((tm, 3 * ATTN_WIDTH), rows),
            pl.BlockSpec((tm, cw), rows),
        ],
        out_shape=[
            jax.ShapeDtypeStruct((B * T, 3 * ATTN_WIDTH), BF16),
            jax.ShapeDtypeStruct((B * T, cw), BF16),
        ],
        scratch_shapes=[
            pltpu.VMEM((tm, D), BF16),
            pltpu.VMEM((cw // LANES, tm + CONV_HALO, LANES), F32),
            pltpu.VMEM((cw // LANES, CONV_HALO, LANES), F32),
            pltpu.VMEM((2, SUBLANES - 1, tm + CONV_HALO, LANES), F32),
            pltpu.VMEM((tm, cw), F32),
        ],
        compiler_params=_params("arbitrary", "arbitrary"),
        name="inproj",
    )(x, g_mix.reshape(1, D), sh_m, sc_m, w_in_bf, rope_tab,
      conv_w, conv_b.reshape(1, cw), ln_g.reshape(1, cw), ln_b.reshape(1, cw))


def _attn_kernel(q_ref, k_ref, v_ref, *rest):
    n_w = (len(rest) - 4) // 2
    w_refs, o_ref, wbf_refs = rest[:n_w], rest[n_w], rest[n_w + 1:2 * n_w + 1]
    vt_scr, st_scr, pt_scr = rest[2 * n_w + 1:]
    for w_ref, wbf_ref in zip(w_refs, wbf_refs):
        wbf_ref[...] = w_ref[...].astype(BF16)

    T = q_ref.shape[0]
    BLK = MOBA_BLOCK
    nb = T // BLK
    nh = q_ref.shape[1] // HEAD_DIM
    ones_rows = vt_scr.shape[1] - HEAD_DIM
    topk = min(MOBA_TOPK, nb - 1)

    row = lax.broadcasted_iota(jnp.int32, (BLK, BLK), 0)
    col = lax.broadcasted_iota(jnp.int32, (BLK, BLK), 1)
    causal = row <= col
    blk_id = lax.broadcasted_iota(jnp.int32, (nb, BLK), 0)

    def prologue(h):
        hd = slice(h * HEAD_DIM, (h + 1) * HEAD_DIM)
        vt_scr[h, 0:HEAD_DIM, :] = v_ref[:, hd].astype(F32).T.astype(BF16)
        vt_scr[h, HEAD_DIM:, :] = jnp.ones((ones_rows, T), BF16)
        kmean = jnp.mean(k_ref[:, hd].astype(F32).reshape(nb, BLK, HEAD_DIM), axis=1)
        km_hi = kmean.astype(BF16).astype(F32)
        km2 = jnp.concatenate([km_hi, kmean - km_hi], axis=0).astype(BF16)
        g2 = lax.dot_general(km2, q_ref[:, hd], NT_DIMS, preferred_element_type=F32)
        return g2[0:nb, :] + g2[nb:2 * nb, :]

    def scores(h, gate, i, slot):
        hd = slice(h * HEAD_DIM, (h + 1) * HEAD_DIM)
        qi = q_ref[i * BLK:(i + 1) * BLK, hd]
        if i <= topk:
            bias = None
        else:
            g = gate[:, i * BLK:(i + 1) * BLK]
            rank = jnp.zeros((nb, BLK), jnp.int32)
            for n2 in range(i):
                gn = g[n2:n2 + 1, :]
                beats = (gn > g) | ((gn == g) & (n2 < blk_id))
                rank = rank + beats.astype(jnp.int32)
            bias = jnp.where(rank < topk, 0.0, -MASK_BIG).astype(F32)
        m = None
        for j in range(i + 1):
            s = lax.dot_general(k_ref[j * BLK:(j + 1) * BLK, hd], qi, NT_DIMS, preferred_element_type=F32)
            if j == i:
                s = jnp.where(causal, s, -MASK_BIG)
            st_scr[slot, j * BLK:(j + 1) * BLK, :] = s
            tmax = jnp.max(s, axis=0, keepdims=True)
            if bias is not None and j < i:
                tmax = tmax + bias[j:j + 1, :]
            m = tmax if m is None else jnp.maximum(m, tmax)
        return m, bias

    def probs(i, slot, m, bias):
        for j in range(i + 1):
            c = m
            if bias is not None and j < i:
                c = m - bias[j:j + 1, :]
            p = jnp.exp2(st_scr[slot, j * BLK:(j + 1) * BLK, :] - c)
            pt_scr[slot, j * BLK:(j + 1) * BLK, :] = p.astype(BF16)

    def output(h, i, slot):
        kv = (i + 1) * BLK
        acc = jnp.dot(vt_scr[h, :, 0:kv], pt_scr[slot, 0:kv, :], preferred_element_type=F32)
        inv_l = 1.0 / acc[HEAD_DIM:HEAD_DIM + 1, :]
        o_ref[i * BLK:(i + 1) * BLK, h * HEAD_DIM:(h + 1) * HEAD_DIM] = (acc[0:HEAD_DIM, :] * inv_l).T.astype(o_ref.dtype)

    pairs = []
    lo, hi = 0, nb - 1
    while lo < hi:
        pairs.append((lo, hi))
        lo, hi = lo + 1, hi - 1
    if lo == hi:
        pairs.append((lo,))
    gates = [prologue(h) for h in range(nh)]
    for group in pairs:
        chains = [(h, i) for h in range(nh) for i in group]
        stats = [scores(h, gates[h], i, slot) for slot, (h, i) in enumerate(chains)]
        for slot, (h, i) in enumerate(chains):
            probs(i, slot, *stats[slot])
        for slot, (h, i) in enumerate(chains):
            output(h, i, slot)


def _attn(qkv, B, T, weights):
    H = ATTN_HEADS
    nh = ATTN_HEADS_PER_STEP
    hw = nh * HEAD_DIM
    ones_rows = 16
    steps = B * (H // nh)
    for w in weights:
        assert w.shape[0] % (steps * 2 * SUBLANES) == 0, w.shape
    flat = lambda b, h: (b * (H // nh) + h, 0)
    w_specs = [pl.BlockSpec((w.shape[0] // steps, w.shape[1]), flat) for w in weights]
    outs = pl.pallas_call(
        _attn_kernel,
        grid=(B, H // nh),
        in_specs=[
            pl.BlockSpec((T, hw), lambda b, h: (b, h)),
            pl.BlockSpec((T, hw), lambda b, h: (b, H // nh + h)),
            pl.BlockSpec((T, hw), lambda b, h: (b, 2 * (H // nh) + h)),
        ] + w_specs,
        out_specs=[pl.BlockSpec((T, hw), lambda b, h: (b, h))] + w_specs,
        out_shape=[jax.ShapeDtypeStruct((B * T, ATTN_WIDTH), BF16)]
                  + [jax.ShapeDtypeStruct(w.shape, BF16) for w in weights],
        scratch_shapes=[
            pltpu.VMEM((nh, HEAD_DIM + ones_rows, T), BF16),
            pltpu.VMEM((2 * nh, T, MOBA_BLOCK), F32),
            pltpu.VMEM((2 * nh, T, MOBA_BLOCK), BF16),
        ],
        compiler_params=_params("arbitrary", "arbitrary", vmem_limit=BIG_VMEM_LIMIT),
        name="attn",
    )(qkv, qkv, qkv, *weights)
    return outs[0], outs[1:]


def _outproj_kernel(attn_ref, conv_ref, w_ref, x_ref, gt_ref, g_ref, sh_ref, sc_ref, h_ref, u_ref):
    aw = attn_ref.shape[1]
    mixed = (jnp.dot(attn_ref[...], w_ref[0:aw, :], preferred_element_type=F32)
             + jnp.dot(conv_ref[...], w_ref[aw:, :], preferred_element_type=F32))
    h = x_ref[0] + gt_ref[0] * mixed
    h_ref[...] = h
    u_ref[...] = _norm_modulate(h, g_ref[...], sh_ref[0], sc_ref[0]).astype(BF16)


def _outproj(attn, conv, w_out_bf, x, gt_m, g_ffn, sh_f, sc_f):
    B, T, D = x.shape
    aw, cw = attn.shape[1], conv.shape[1]
    tm = ROW_TILE
    nt = T // tm
    vec = pl.BlockSpec((1, 1, D), lambda b, i: (b, 0, 0))
    rows = lambda b, i: (b * nt + i, 0)
    return pl.pallas_call(
        _outproj_kernel,
        grid=(B, nt),
        in_specs=[
            pl.BlockSpec((tm, aw), rows),
            pl.BlockSpec((tm, cw), rows),
            pl.BlockSpec((D, D), lambda b, i: (0, 0), pipeline_mode=pl.Buffered(1)),
            pl.BlockSpec((1, tm, D), lambda b, i: (b, i, 0)),
            vec,
            pl.BlockSpec((1, D), lambda b, i: (0, 0)),
            vec, vec,
        ],
        out_specs=[pl.BlockSpec((tm, D), rows), pl.BlockSpec((tm, D), rows)],
        out_shape=[jax.ShapeDtypeStruct((B * T, D), F32), jax.ShapeDtypeStruct((B * T, D), BF16)],
        compiler_params=_params("arbitrary", "arbitrary"),
        name="outproj",
    )(attn, conv, w_out_bf, x, gt_m, g_ffn.reshape(1, D), sh_f, sc_f)


def _ffn_kernel(u_ref, wg_ref, wu_ref, wd_ref, h_hbm, gt_ref, g_ref, o_ref, h_scr, h_sem):
    b, i, f = pl.program_id(0), pl.program_id(1), pl.program_id(2)
    tm = u_ref.shape[0]
    nf = pl.num_programs(2)
    row0 = pl.multiple_of((b * pl.num_programs(1) + i) * tm, tm)
    h_copy = pltpu.make_async_copy(h_hbm.at[pl.ds(row0, tm), :], h_scr, h_sem.at[0])

    @pl.when(f == 0)
    def _():
        h_copy.start()
        o_ref[0] = jnp.zeros(o_ref.shape[1:], F32)

    u = u_ref[...]
    gte = jnp.dot(u, wg_ref[...], preferred_element_type=F32)
    up = jnp.dot(u, wu_ref[...], preferred_element_type=F32)
    a = (gte * jax.nn.sigmoid(gte) * up).astype(BF16)
    D = o_ref.shape[2]
    for n0 in range(0, D, FFN_DOWN_COLS):
        cols = slice(n0, n0 + FFN_DOWN_COLS)
        o_ref[0, :, cols] += jnp.dot(a, wd_ref[:, cols], preferred_element_type=F32)

    @pl.when(f == nf - 1)
    def _():
        h_copy.wait()
        gt = gt_ref[0]
        g = g_ref[...]
        for r0 in range(0, tm, FFN_EPI_ROWS):
            rows = slice(r0, r0 + FFN_EPI_ROWS)
            h = h_scr[rows, :] + gt * o_ref[0, rows, :]
            ms = jnp.mean(h * h, axis=-1, keepdims=True)
            o_ref[0, rows, :] = h * lax.rsqrt(ms + EPS) * g


def _ffn(u2, wg_bf, wu_bf, wd_bf, h1, gt_f, g_final, B, T):
    D = u2.shape[1]
    FF = wg_bf.shape[1]
    tm, tf = FFN_ROW_TILE, FF_TILE
    nt = T // tm
    rows = lambda b, i, f: (b * nt + i, 0)
    return pl.pallas_call(
        _ffn_kernel,
        grid=(B, nt, FF // tf),
        in_specs=[
            pl.BlockSpec((tm, D), rows),
            pl.BlockSpec((D, tf), lambda b, i, f: (0, f)),
            pl.BlockSpec((D, tf), lambda b, i, f: (0, f)),
            pl.BlockSpec((tf, D), lambda b, i, f: (f, 0)),
            pl.BlockSpec(memory_space=pl.ANY),
            pl.BlockSpec((1, 1, D), lambda b, i, f: (b, 0, 0)),
            pl.BlockSpec((1, D), lambda b, i, f: (0, 0)),
        ],
        out_specs=pl.BlockSpec((1, tm, D), lambda b, i, f: (b, i, 0)),
        out_shape=jax.ShapeDtypeStruct((B, T, D), F32),
        scratch_shapes=[pltpu.VMEM((tm, D), F32), pltpu.SemaphoreType.DMA((1,))],
        compiler_params=_params("arbitrary", "arbitrary", "arbitrary", vmem_limit=BIG_VMEM_LIMIT),
        name="ffn",
    )(u2, wg_bf, wu_bf, wd_bf, h1, gt_f, g_final.reshape(1, D))


def _rope_tables(T):
    half = HEAD_DIM // 2
    inv = ROPE_THETA ** (-jnp.arange(half, dtype=F32) / half)
    ang = jnp.arange(T, dtype=F32)[:, None] * inv[None, :]
    cos, sin = jnp.cos(ang), jnp.sin(ang)
    cos_f = jnp.concatenate([cos, cos], axis=-1)
    sin_f = jnp.concatenate([-sin, sin], axis=-1)
    qs = (HEAD_DIM ** -0.5) * math.log2(math.e)
    return jnp.concatenate([cos_f * qs, sin_f * qs, cos_f, sin_f], axis=-1)


def kernel(x, c, w_ada, b_ada, g_mix, w_in, conv_w, conv_b, ln_g, ln_b, w_out, g_ffn, w_gate, w_up, w_down, g_final):
    B, T, D = x.shape
    depth = w_ada.shape[0]
    assert depth == 1 and T % MOBA_BLOCK == 0 and T % ROW_TILE == 0 and T % FFN_ROW_TILE == 0 and D % LANES == 0
    rope_tab = _rope_tables(T)
    h = x
    for l in range(depth):
        mod, w_in_bf = _ada(c, w_ada[l], b_ada[l], w_in[l])
        sh_m, sc_m, gt_m, sh_f, sc_f, gt_f = [m.reshape(B, 1, D) for m in jnp.split(mod, N_MOD, axis=-1)]
        qkv, conv = _inproj(h, g_mix[l], sh_m, sc_m, w_in_bf, rope_tab,
                            conv_w[l], conv_b[l], ln_g[l], ln_b[l])
        attn, (w_out_bf, w_gate_bf, w_up_bf, w_down_bf) = _attn(
            qkv, B, T, (w_out[l], w_gate[l], w_up[l], w_down[l]))
        h1, u2 = _outproj(attn, conv, w_out_bf, h, gt_m, g_ffn[l], sh_f, sc_f)
        out = _ffn(u2, w_gate_bf, w_up_bf, w_down_bf, h1, gt_f, g_final, B, T)
    return out
```

```python
import math

import jax
import jax.numpy as jnp
from jax import lax
from jax.experimental import pallas as pl
from jax.experimental.pallas import tpu as pltpu

ATTN_HEADS = 8
HEAD_DIM = 128
ATTN_WIDTH = ATTN_HEADS * HEAD_DIM
CONV_KERNEL = 31
MOBA_BLOCK = 256
MOBA_TOPK = 3
ROPE_THETA = 10000.0
N_MOD = 6
EPS = 1e-6

F32 = jnp.float32
BF16 = jnp.bfloat16

SUBLANES = 8
LANES = 128
MASK_BIG = 1e30
VMEM_LIMIT = 56 * 1024 * 1024
BIG_VMEM_LIMIT = 60 * 1024 * 1024

ADA_TN = 768
ROW_TILE = 512
FFN_ROW_TILE = 1024
FF_TILE = 512
ATTN_HEADS_PER_STEP = 4
CONV_HALO = 32
CONV_CHUNK = 64

NT_DIMS = (((1,), (1,)), ((), ()))


def _params(*sem, vmem_limit=VMEM_LIMIT):
    return pltpu.CompilerParams(dimension_semantics=sem, vmem_limit_bytes=vmem_limit)


def _ada_kernel(c_ref, w_ref, b_ref, wi_ref, o_ref, wi_bf_ref):
    c = c_ref[...]
    cs = (c * jax.nn.sigmoid(c)).astype(BF16)
    o_ref[...] = jnp.dot(cs, w_ref[...].astype(BF16), preferred_element_type=F32) + b_ref[...]
    wi_bf_ref[...] = wi_ref[...].astype(BF16)


def _ada(c, w_ada, b_ada, w_in):
    B, D = c.shape
    N = w_ada.shape[1]
    steps = N // ADA_TN
    assert N % ADA_TN == 0 and w_in.shape[0] % (steps * 2 * SUBLANES) == 0
    wi_spec = pl.BlockSpec((w_in.shape[0] // steps, w_in.shape[1]), lambda j: (j, 0))
    return pl.pallas_call(
        _ada_kernel,
        grid=(steps,),
        in_specs=[
            pl.BlockSpec((B, D), lambda j: (0, 0)),
            pl.BlockSpec((D, ADA_TN), lambda j: (0, j)),
            pl.BlockSpec((1, ADA_TN), lambda j: (0, j)),
            wi_spec,
        ],
        out_specs=[pl.BlockSpec((B, ADA_TN), lambda j: (0, j)), wi_spec],
        out_shape=[jax.ShapeDtypeStruct((B, N), F32), jax.ShapeDtypeStruct(w_in.shape, BF16)],
        compiler_params=_params("arbitrary"),
        name="ada",
    )(c, w_ada, b_ada.reshape(1, N), w_in)


def _norm_modulate(x, g, shift, scale):
    ms = jnp.mean(x * x, axis=-1, keepdims=True)
    y = x * lax.rsqrt(ms + EPS) * g
    return y * (1.0 + scale) + shift


def _inproj_kernel(x_ref, g_ref, sh_ref, sc_ref, w_ref, rope_ref, cw_ref, cb_ref, lg_ref, lb_ref,
                   qkv_ref, conv_ref, u_scr, win_scr, halo_scr, shift_scr, y_scr):
    W = ATTN_WIDTH
    tm = x_ref.shape[1]
    C = conv_ref.shape[1]
    ns = C // LANES

    @pl.when(pl.program_id(1) == 0)
    def _():
        halo_scr[...] = jnp.zeros(halo_scr.shape, F32)

    u_scr[...] = _norm_modulate(x_ref[0], g_ref[...], sh_ref[0], sc_ref[0]).astype(BF16)

    a = jnp.dot(u_scr[...], w_ref[:, 3 * W:3 * W + C], preferred_element_type=F32)
    b = jnp.dot(u_scr[...], w_ref[:, 3 * W + C:3 * W + 2 * C], preferred_element_type=F32)
    glu = a * jax.nn.sigmoid(b)
    for cs in range(ns):
        lanes = slice(cs * LANES, (cs + 1) * LANES)
        win_scr[cs, 0:CONV_HALO, :] = halo_scr[cs]
        win_scr[cs, CONV_HALO:CONV_HALO + tm, :] = glu[:, lanes]
        halo_scr[cs] = glu[tm - CONV_HALO:tm, lanes]

    for part in range(2):
        p = jnp.dot(u_scr[...], w_ref[:, part * W:(part + 1) * W], preferred_element_type=F32)
        cos = rope_ref[:, (2 * part) * HEAD_DIM:(2 * part + 1) * HEAD_DIM]
        sin = rope_ref[:, (2 * part + 1) * HEAD_DIM:(2 * part + 2) * HEAD_DIM]
        for h in range(ATTN_HEADS):
            ph = p[:, h * HEAD_DIM:(h + 1) * HEAD_DIM]
            r = ph * cos + pltpu.roll(ph, HEAD_DIM // 2, axis=1) * sin
            qkv_ref[:, part * W + h * HEAD_DIM: part * W + (h + 1) * HEAD_DIM] = r.astype(BF16)
    v = jnp.dot(u_scr[...], w_ref[:, 2 * W:3 * W], preferred_element_type=F32)
    qkv_ref[:, 2 * W:3 * W] = v.astype(BF16)

    ext = tm + CONV_HALO - SUBLANES
    lead = CONV_HALO - (CONV_KERNEL - 1)
    for cs in range(ns):
        lanes = slice(cs * LANES, (cs + 1) * LANES)
        slot = cs % 2
        for r in range(1, SUBLANES):
            shift_scr[slot, r - 1, 0:ext, :] = win_scr[cs, r:r + ext, :]
        taps_w = [jnp.broadcast_to(cw_ref[j:j + 1, lanes], (CONV_CHUNK, LANES)) for j in range(CONV_KERNEL)]
        bias = jnp.broadcast_to(cb_ref[:, lanes], (CONV_CHUNK, LANES))
        for ci in range(tm // CONV_CHUNK):
            acc = bias
            for j in range(CONV_KERNEL):
                s = lead + j
                r, t0 = s % SUBLANES, ci * CONV_CHUNK + (s // SUBLANES) * SUBLANES
                if r == 0:
                    tap = win_scr[cs, t0:t0 + CONV_CHUNK, :]
                else:
                    tap = shift_scr[slot, r - 1, t0:t0 + CONV_CHUNK, :]
                acc = acc + tap * taps_w[j]
            y_scr[ci * CONV_CHUNK:(ci + 1) * CONV_CHUNK, lanes] = acc

    y = y_scr[...]
    mu = jnp.mean(y, axis=-1, keepdims=True)
    d = y - mu
    var = jnp.mean(d * d, axis=-1, keepdims=True)
    yn = d * lax.rsqrt(var + EPS) * lg_ref[...] + lb_ref[...]
    conv_ref[...] = (yn * jax.nn.sigmoid(yn)).astype(conv_ref.dtype)


def _inproj(x, g_mix, sh_m, sc_m, w_in_bf, rope_tab, conv_w, conv_b, ln_g, ln_b):
    B, T, D = x.shape
    NW = w_in_bf.shape[1]
    cw = (NW - 3 * ATTN_WIDTH) // 2
    tm = ROW_TILE
    nt = T // tm
    vec = pl.BlockSpec((1, 1, D), lambda b, i: (b, 0, 0))
    cvec = pl.BlockSpec((1, cw), lambda b, i: (0, 0))
    rows = lambda b, i: (b * nt + i, 0)
    return pl.pallas_call(
        _inproj_kernel,
        grid=(B, nt),
        in_specs=[
            pl.BlockSpec((1, tm, D), lambda b, i: (b, i, 0)),
            pl.BlockSpec((1, D), lambda b, i: (0, 0)),
            vec, vec,
            pl.BlockSpec((D, NW), lambda b, i: (0, 0), pipeline_mode=pl.Buffered(1)),
            pl.BlockSpec((tm, 4 * HEAD_DIM), lambda b, i: (i, 0)),
            pl.BlockSpec((CONV_KERNEL, cw), lambda b, i: (0, 0)),
            cvec, cvec, cvec,
        ],
        out_specs=[
            pl.BlockSpec((tm, 3 * ATTN_WIDTH), rows),
            pl.BlockSpec((tm, cw), rows),
        ],
        out_shape=[
            jax.ShapeDtypeStruct((B * T, 3 * ATTN_WIDTH), BF16),
            jax.ShapeDtypeStruct((B * T, cw), BF16),
        ],
        scratch_shapes=[
            pltpu.VMEM((tm, D), BF16),
            pltpu.VMEM((cw // LANES, tm + CONV_HALO, LANES), F32),
            pltpu.VMEM((cw // LANES, CONV_HALO, LANES), F32),
            pltpu.VMEM((2, SUBLANES - 1, tm + CONV_HALO, LANES), F32),
            pltpu.VMEM((tm, cw), F32),
        ],
        compiler_params=_params("arbitrary", "arbitrary"),
        name="inproj",
    )(x, g_mix.reshape(1, D), sh_m, sc_m, w_in_bf, rope_tab,
      conv_w, conv_b.reshape(1, cw), ln_g.reshape(1, cw), ln_b.reshape(1, cw))


def _attn_kernel(q_ref, k_ref, v_ref, *rest):
    n_w = (len(rest) - 4) // 2
    w_refs, o_ref, wbf_refs = rest[:n_w], rest[n_w], rest[n_w + 1:2 * n_w + 1]
    vt_scr, st_scr, pt_scr = rest[2 * n_w + 1:]
    for w_ref, wbf_ref in zip(w_refs, wbf_refs):
        wbf_ref[...] = w_ref[...].astype(BF16)

    T = q_ref.shape[0]
    BLK = MOBA_BLOCK
    nb = T // BLK
    nh = q_ref.shape[1] // HEAD_DIM
    ones_rows = vt_scr.shape[1] - HEAD_DIM
    topk = min(MOBA_TOPK, nb - 1)

    row = lax.broadcasted_iota(jnp.int32, (BLK, BLK), 0)
    col = lax.broadcasted_iota(jnp.int32, (BLK, BLK), 1)
    causal = row <= col
    blk_id = lax.broadcasted_iota(jnp.int32, (nb, BLK), 0)

    def prologue(h):
        hd = slice(h * HEAD_DIM, (h + 1) * HEAD_DIM)
        vt_scr[h, 0:HEAD_DIM, :] = v_ref[:, hd].astype(F32).T.astype(BF16)
        vt_scr[h, HEAD_DIM:, :] = jnp.ones((ones_rows, T), BF16)
        kmean = jnp.mean(k_ref[:, hd].astype(F32).reshape(nb, BLK, HEAD_DIM), axis=1)
        km_hi = kmean.astype(BF16).astype(F32)
        km2 = jnp.concatenate([km_hi, kmean - km_hi], axis=0).astype(BF16)
        g2 = lax.dot_general(km2, q_ref[:, hd], NT_DIMS, preferred_element_type=F32)
        return g2[0:nb, :] + g2[nb:2 * nb, :]

    def scores(h, gate, i, slot):
        hd = slice(h * HEAD_DIM, (h + 1) * HEAD_DIM)
        qi = q_ref[i * BLK:(i + 1) * BLK, hd]
        if i <= topk:
            bias = None
        else:
            g = gate[:, i * BLK:(i + 1) * BLK]
            rank = jnp.zeros((nb, BLK), jnp.int32)
            for n2 in range(i):
                gn = g[n2:n2 + 1, :]
                beats = (gn > g) | ((gn == g) & (n2 < blk_id))
                rank = rank + beats.astype(jnp.int32)
            bias = jnp.where(rank < topk, 0.0, -MASK_BIG).astype(F32)
        m = None
        for j in range(i + 1):
            s = lax.dot_general(k_ref[j * BLK:(j + 1) * BLK, hd], qi, NT_DIMS, preferred_element_type=F32)
            if j == i:
                s = jnp.where(causal, s, -MASK_BIG)
            st_scr[slot, j * BLK:(j + 1) * BLK, :] = s
            tmax = jnp.max(s, axis=0, keepdims=True)
            if bias is not None and j < i:
                tmax = tmax + bias[j:j + 1, :]
            m = tmax if m is None else jnp.maximum(m, tmax)
        return m, bias

    def probs(i, slot, m, bias):
        for j in range(i + 1):
            c = m
            if bias is not None and j < i:
                c = m - bias[j:j + 1, :]
            p = jnp.exp2(st_scr[slot, j * BLK:(j + 1) * BLK, :] - c)
            pt_scr[slot, j * BLK:(j + 1) * BLK, :] = p.astype(BF16)

    def output(h, i, slot):
        kv = (i + 1) * BLK
        acc = jnp.dot(vt_scr[h, :, 0:kv], pt_scr[slot, 0:kv, :], preferred_element_type=F32)
        inv_l = 1.0 / acc[HEAD_DIM:HEAD_DIM + 1, :]
        o_ref[i * BLK:(i + 1) * BLK, h * HEAD_DIM:(h + 1) * HEAD_DIM] = (acc[0:HEAD_DIM, :] * inv_l).T.astype(o_ref.dtype)

    pairs = []
    lo, hi = 0, nb - 1
    while lo < hi:
        pairs.append((lo, hi))
        lo, hi = lo + 1, hi - 1
    if lo == hi:
        pairs.append((lo,))
    gates = [prologue(h) for h in range(nh)]
    for group in pairs:
        chains = [(h, i) for h in range(nh) for i in group]
        stats = [scores(h, gates[h], i, slot) for slot, (h, i) in enumerate(chains)]
        for slot, (h, i) in enumerate(chains):
            probs(i, slot, *stats[slot])
        for slot, (h, i) in enumerate(chains):
            output(h, i, slot)


def _attn(qkv, B, T, weights):
    H = ATTN_HEADS
    nh = ATTN_HEADS_PER_STEP
    hw = nh * HEAD_DIM
    ones_rows = 16
    steps = B * (H // nh)
    for w in weights:
        assert w.shape[0] % (steps * 2 * SUBLANES) == 0, w.shape
    flat = lambda b, h: (b * (H // nh) + h, 0)
    w_specs = [pl.BlockSpec((w.shape[0] // steps, w.shape[1]), flat) for w in weights]
    outs = pl.pallas_call(
        _attn_kernel,
        grid=(B, H // nh),
        in_specs=[
            pl.BlockSpec((T, hw), lambda b, h: (b, h)),
            pl.BlockSpec((T, hw), lambda b, h: (b, H // nh + h)),
            pl.BlockSpec((T, hw), lambda b, h: (b, 2 * (H // nh) + h)),
        ] + w_specs,
        out_specs=[pl.BlockSpec((T, hw), lambda b, h: (b, h))] + w_specs,
        out_shape=[jax.ShapeDtypeStruct((B * T, ATTN_WIDTH), BF16)]
                  + [jax.ShapeDtypeStruct(w.shape, BF16) for w in weights],
        scratch_shapes=[
            pltpu.VMEM((nh, HEAD_DIM + ones_rows, T), BF16),
            pltpu.VMEM((2 * nh, T, MOBA_BLOCK), F32),
            pltpu.VMEM((2 * nh, T, MOBA_BLOCK), BF16),
        ],
        compiler_params=_params("arbitrary", "arbitrary", vmem_limit=BIG_VMEM_LIMIT),
        name="attn",
    )(qkv, qkv, qkv, *weights)
    return outs[0], outs[1:]


def _outproj_kernel(attn_ref, conv_ref, w_ref, x_ref, gt_ref, g_ref, sh_ref, sc_ref, h_ref, u_ref):
    aw = attn_ref.shape[1]
    mixed = (jnp.dot(attn_ref[...], w_ref[0:aw, :], preferred_element_type=F32)
             + jnp.dot(conv_ref[...], w_ref[aw:, :], preferred_element_type=F32))
    h = x_ref[0] + gt_ref[0] * mixed
    h_ref[...] = h
    u_ref[...] = _norm_modulate(h, g_ref[...], sh_ref[0], sc_ref[0]).astype(BF16)


def _outproj(attn, conv, w_out_bf, x, gt_m, g_ffn, sh_f, sc_f):
    B, T, D = x.shape
    aw, cw = attn.shape[1], conv.shape[1]
    tm = ROW_TILE
    nt = T // tm
    vec = pl.BlockSpec((1, 1, D), lambda b, i: (b, 0, 0))
    rows = lambda b, i: (b * nt + i, 0)
    return pl.pallas_call(
        _outproj_kernel,
        grid=(B, nt),
        in_specs=[
            pl.BlockSpec((tm, aw), rows),
            pl.BlockSpec((tm, cw), rows),
            pl.BlockSpec((D, D), lambda b, i: (0, 0), pipeline_mode=pl.Buffered(1)),
            pl.BlockSpec((1, tm, D), lambda b, i: (b, i, 0)),
            vec,
            pl.BlockSpec((1, D), lambda b, i: (0, 0)),
            vec, vec,
        ],
        out_specs=[pl.BlockSpec((tm, D), rows), pl.BlockSpec((tm, D), rows)],
        out_shape=[jax.ShapeDtypeStruct((B * T, D), F32), jax.ShapeDtypeStruct((B * T, D), BF16)],
        compiler_params=_params("arbitrary", "arbitrary"),
        name="outproj",
    )(attn, conv, w_out_bf, x, gt_m, g_ffn.reshape(1, D), sh_f, sc_f)


def _ffn_up_kernel(u_ref, wg_ref, wu_ref, a_ref):
    u = u_ref[...]
    gte = jnp.dot(u, wg_ref[...], preferred_element_type=F32)
    up = jnp.dot(u, wu_ref[...], preferred_element_type=F32)
    a_ref[...] = (gte * jax.nn.sigmoid(gte) * up).astype(BF16)


def _ffn_down_kernel(a_ref, wd_ref, h_ref, gt_ref, g_ref, o_ref):
    ff = jnp.dot(a_ref[...], wd_ref[...], preferred_element_type=F32)
    h = h_ref[...] + gt_ref[0] * ff
    ms = jnp.mean(h * h, axis=-1, keepdims=True)
    o_ref[0] = h * lax.rsqrt(ms + EPS) * g_ref[...]


def _ffn(u2, wg_bf, wu_bf, wd_bf, h1, gt_f, g_final, B, T):
    N, D = u2.shape
    FF = wg_bf.shape[1]
    tm, tf = FFN_ROW_TILE, FF_TILE
    act = pl.pallas_call(
        _ffn_up_kernel,
        grid=(N // tm, FF // tf),
        in_specs=[
            pl.BlockSpec((tm, D), lambda i, f: (i, 0)),
            pl.BlockSpec((D, tf), lambda i, f: (0, f)),
            pl.BlockSpec((D, tf), lambda i, f: (0, f)),
        ],
        out_specs=pl.BlockSpec((tm, tf), lambda i, f: (i, f)),
        out_shape=jax.ShapeDtypeStruct((N, FF), BF16),
        compiler_params=_params("arbitrary", "arbitrary"),
        name="ffn_up",
    )(u2, wg_bf, wu_bf)
    td = ROW_TILE
    nt = T // td
    rows = lambda b, i: (b * nt + i, 0)
    return pl.pallas_call(
        _ffn_down_kernel,
        grid=(B, nt),
        in_specs=[
            pl.BlockSpec((td, FF), rows),
            pl.BlockSpec((FF, D), lambda b, i: (0, 0), pipeline_mode=pl.Buffered(1)),
            pl.BlockSpec((td, D), rows),
            pl.BlockSpec((1, 1, D), lambda b, i: (b, 0, 0)),
            pl.BlockSpec((1, D), lambda b, i: (0, 0)),
        ],
        out_specs=pl.BlockSpec((1, td, D), lambda b, i: (b, i, 0)),
        out_shape=jax.ShapeDtypeStruct((B, T, D), F32),
        compiler_params=_params("arbitrary", "arbitrary"),
        name="ffn_down",
    )(act, wd_bf, h1, gt_f, g_final.reshape(1, D))


def _rope_tables(T):
    half = HEAD_DIM // 2
    inv = ROPE_THETA ** (-jnp.arange(half, dtype=F32) / half)
    ang = jnp.arange(T, dtype=F32)[:, None] * inv[None, :]
    cos, sin = jnp.cos(ang), jnp.sin(ang)
    cos_f = jnp.concatenate([cos, cos], axis=-1)
    sin_f = jnp.concatenate([-sin, sin], axis=-1)
    qs = (HEAD_DIM ** -0.5) * math.log2(math.e)
    return jnp.concatenate([cos_f * qs, sin_f * qs, cos_f, sin_f], axis=-1)


def kernel(x, c, w_ada, b_ada, g_mix, w_in, conv_w, conv_b, ln_g, ln_b, w_out, g_ffn, w_gate, w_up, w_down, g_final):
    B, T, D = x.shape
    depth = w_ada.shape[0]
    assert depth == 1 and T % MOBA_BLOCK == 0 and T % ROW_TILE == 0 and T % FFN_ROW_TILE == 0 and D % LANES == 0
    rope_tab = _rope_tables(T)
    h = x
    for l in range(depth):
        mod, w_in_bf = _ada(c, w_ada[l], b_ada[l], w_in[l])
        sh_m, sc_m, gt_m, sh_f, sc_f, gt_f = [m.reshape(B, 1, D) for m in jnp.split(mod, N_MOD, axis=-1)]
        qkv, conv = _inproj(h, g_mix[l], sh_m, sc_m, w_in_bf, rope_tab,
                            conv_w[l], conv_b[l], ln_g[l], ln_b[l])
        attn, (w_out_bf, w_gate_bf, w_up_bf, w_down_bf) = _attn(
            qkv, B, T, (w_out[l], w_gate[l], w_up[l], w_down[l]))
        h1, u2 = _outproj(attn, conv, w_out_bf, h, gt_m, g_ffn[l], sh_f, sc_f)
        out = _ffn(u2, w_gate_bf, w_up_bf, w_down_bf, h1, gt_f, g_final, B, T)
    return out
```
